```python
import jax, jax.numpy as jnp
from jax import lax
import numpy as np

D_MODEL = 2048
BATCH = 4
SEQ = 2048
DEPTH = 4
DEC_BATCH = 128
DEC_SEQ = 4
PAST_LEN = 16384
PAGE_SIZE = 128

D_A = D_MODEL // 2
D_B = D_MODEL
CONV_A_W = 3
CONV_B_W = 4
N_BLOCKS = 16
BLOCK = D_B // N_BLOCKS
C_LRU = 8.0
FFN_HIDDEN = -(-8 * D_MODEL // (3 * 256)) * 256
EPS = 1e-6
IN_COLS = 3 * D_A + 2 * D_B + 2 * D_MODEL
SPLITS = [D_A, 2 * D_A, 3 * D_A, 3 * D_A + D_B, 3 * D_A + 2 * D_B, 3 * D_A + 2 * D_B + D_MODEL]

kernel_name = "hybrid_conv_rglru_decoder_step"


def rms_norm(x, g):
    xf = x.astype(jnp.float32)
    var = jnp.mean(xf * xf, axis=-1, keepdims=True)
    return (xf * lax.rsqrt(var + EPS) * g.astype(jnp.float32)).astype(x.dtype)


def causal_dwconv(u, buf, w):
    width = w.shape[0]
    t = u.shape[1]
    full = jnp.concatenate([buf.astype(u.dtype), u], axis=1)
    y = full[:, 0:t] * w[0]
    for k in range(1, width):
        y = y + full[:, k:k + t] * w[k]
    return y, full[:, full.shape[1] - (width - 1):]


def block_diag(x, w, b):
    xb = x.reshape(x.shape[0], x.shape[1], N_BLOCKS, BLOCK)
    y = jnp.einsum('bthi,hij->bthj', xb, w)
    return y.reshape(x.shape) + b


def rg_lru(x, h0, w_r, b_r, w_i, b_i, lam):
    r = jax.nn.sigmoid(block_diag(x, w_r, b_r).astype(jnp.float32))
    i = jax.nn.sigmoid(block_diag(x, w_i, b_i).astype(jnp.float32))
    log_a = C_LRU * r * jax.nn.log_sigmoid(lam.astype(jnp.float32))
    a = jnp.exp(log_a)
    mult = jnp.sqrt(-jnp.expm1(2.0 * log_a))
    b = mult * i * x.astype(jnp.float32)
    b = b.at[:, 0].add(a[:, 0] * h0.astype(jnp.float32))

    def combine(left, right):
        a1, b1 = left
        a2, b2 = right
        return a1 * a2, a2 * b1 + b2

    _, h = lax.associative_scan(combine, (a, b), axis=1)
    return h.astype(x.dtype), h[:, -1].astype(h0.dtype)


def hybrid_layer(x, buf_a, buf_b, h0, w_in, conv_a_w, w_out_a, conv_b_w, conv_b_bias,
                 w_r, b_r, w_i, b_i, lam, w_out_b, w_o, g_pre_mix, g_post_mix,
                 g_pre_ffn, g_post_ffn, w_gate_up, w_down):
    u = rms_norm(x, g_pre_mix)
    proj = jnp.einsum('btd,dp->btp', u, w_in)
    a_bg, a_cg, a_x, b_x, b_gate, gate_a, gate_b = jnp.split(proj, SPLITS, axis=-1)
    conv_out, new_a = causal_dwconv(a_cg * a_x, buf_a, conv_a_w)
    y_a = jnp.einsum('btc,cd->btd', a_bg * conv_out, w_out_a)
    xb, new_b = causal_dwconv(b_x, buf_b, conv_b_w)
    xb = xb + conv_b_bias
    h_seq, h_last = rg_lru(xb, h0, w_r, b_r, w_i, b_i, lam)
    y_b = jnp.einsum('btc,cd->btd', h_seq * jax.nn.gelu(b_gate), w_out_b)
    merged = jax.nn.sigmoid(gate_a) * y_a + jax.nn.sigmoid(gate_b) * y_b
    mix = jnp.einsum('btd,de->bte', merged, w_o)
    x = x + rms_norm(mix, g_post_mix)
    v = rms_norm(x, g_pre_ffn)
    gu = jnp.einsum('btd,df->btf', v, w_gate_up)
    g, up = jnp.split(gu, 2, axis=-1)
    f = jnp.einsum('btf,fd->btd', jax.nn.silu(g) * up, w_down)
    x = x + rms_norm(f, g_post_ffn)
    return x, new_a, new_b, h_last


def setup_inputs(seed: int = 0) -> dict:
    key = jax.random.key(seed)
    ks = jax.random.split(key, 24)
    nrm = lambda k, shape, s: jax.random.normal(k, shape, jnp.float32) * s
    u = jax.random.uniform(ks[10], (DEPTH, D_B), jnp.float32, 0.9, 0.999)
    p = u ** (1.0 / C_LRU)
    lam = jnp.log(p) - jnp.log1p(-p)
    return {
        "x_prompt": nrm(ks[0], (BATCH, SEQ, D_MODEL), 1.0),
        "x_sample": nrm(ks[1], (DEC_BATCH, DEC_SEQ, D_MODEL), 1.0),
        "state_conv_a": nrm(ks[2], (DEPTH, DEC_BATCH, CONV_A_W - 1, D_A), 1.0),
        "state_conv_b": nrm(ks[3], (DEPTH, DEC_BATCH, CONV_B_W - 1, D_B), 1.0),
        "state_lru_h": nrm(ks[4], (DEPTH, DEC_BATCH, D_B), 0.5),
        "w_in": nrm(ks[5], (DEPTH, D_MODEL, IN_COLS), D_MODEL ** -0.5),
        "conv_a_w": nrm(ks[6], (DEPTH, CONV_A_W, D_A), CONV_A_W ** -0.5),
        "w_out_a": nrm(ks[7], (DEPTH, D_A, D_MODEL), D_A ** -0.5),
        "conv_b_w": nrm(ks[8], (DEPTH, CONV_B_W, D_B), CONV_B_W ** -0.5),
        "conv_b_bias": nrm(ks[9], (DEPTH, D_B), 0.02),
        "w_r": nrm(ks[11], (DEPTH, N_BLOCKS, BLOCK, BLOCK), BLOCK ** -0.5),
        "b_r": nrm(ks[12], (DEPTH, D_B), 0.02),
        "w_i": nrm(ks[13], (DEPTH, N_BLOCKS, BLOCK, BLOCK), BLOCK ** -0.5),
        "b_i": nrm(ks[14], (DEPTH, D_B), 0.02),
        "lru_lambda": lam,
        "w_out_b": nrm(ks[15], (DEPTH, D_B, D_MODEL), D_B ** -0.5),
        "w_o": nrm(ks[16], (DEPTH, D_MODEL, D_MODEL), D_MODEL ** -0.5),
        "norm_pre_mix": 1.0 + nrm(ks[17], (DEPTH, D_MODEL), 0.02),
        "norm_post_mix": 1.0 + nrm(ks[18], (DEPTH, D_MODEL), 0.02),
        "norm_pre_ffn": 1.0 + nrm(ks[19], (DEPTH, D_MODEL), 0.02),
        "norm_post_ffn": 1.0 + nrm(ks[20], (DEPTH, D_MODEL), 0.02),
        "w_gate_up": nrm(ks[21], (DEPTH, D_MODEL, 2 * FFN_HIDDEN), D_MODEL ** -0.5),
        "w_down": nrm(ks[22], (DEPTH, FFN_HIDDEN, D_MODEL), FFN_HIDDEN ** -0.5),
    }


def reference(x_prompt, x_sample, state_conv_a, state_conv_b, state_lru_h, w_in, conv_a_w,
              w_out_a, conv_b_w, conv_b_bias, w_r, b_r, w_i, b_i, lru_lambda, w_out_b, w_o,
              norm_pre_mix, norm_post_mix, norm_pre_ffn, norm_post_ffn, w_gate_up, w_down):
    bp = x_prompt.shape[0]
    dt = x_prompt.dtype
    zero_a = jnp.zeros((bp, CONV_A_W - 1, D_A), dt)
    zero_b = jnp.zeros((bp, CONV_B_W - 1, D_B), dt)
    zero_h = jnp.zeros((bp, D_B), state_lru_h.dtype)
    xp, xs = x_prompt, x_sample
    pa, pb, ph, sa, sb, sh = [], [], [], [], [], []
    for l in range(DEPTH):
        w = (w_in[l], conv_a_w[l], w_out_a[l], conv_b_w[l], conv_b_bias[l], w_r[l], b_r[l],
             w_i[l], b_i[l], lru_lambda[l], w_out_b[l], w_o[l], norm_pre_mix[l],
             norm_post_mix[l], norm_pre_ffn[l], norm_post_ffn[l], w_gate_up[l], w_down[l])
        xp, na, nb, nh = hybrid_layer(xp, zero_a, zero_b, zero_h, *w)
        pa.append(na); pb.append(nb); ph.append(nh)
        xs, na, nb, nh = hybrid_layer(xs, state_conv_a[l], state_conv_b[l], state_lru_h[l], *w)
        sa.append(na); sb.append(nb); sh.append(nh)
    return (xp, xs, jnp.stack(pa), jnp.stack(pb), jnp.stack(ph),
            jnp.stack(sa), jnp.stack(sb), jnp.stack(sh))
```

```python
import functools

import jax
import jax.numpy as jnp
from jax import lax
from jax.experimental import pallas as pl
from jax.experimental.pallas import tpu as pltpu

F32 = jnp.float32
BF16 = jnp.bfloat16

D_MODEL = 2048
D_A = D_MODEL // 2
D_B = D_MODEL
CONV_A_W = 3
CONV_B_W = 4
HEAD = 128
C_LRU = 8.0
FFN_HIDDEN = 5632
EPS = 1e-6
IN_COLS = 3 * D_A + 2 * D_B + 2 * D_MODEL

OFF_A_BG, OFF_A_CG, OFF_A_X = 0, D_A, 2 * D_A
OFF_B_X = 3 * D_A
OFF_B_GATE = OFF_B_X + D_B
OFF_GATE_A = OFF_B_GATE + D_B
OFF_GATE_B = OFF_GATE_A + D_MODEL

SUBLANES = 8
TM = 512
TN_IN = 1024
TC_MERGE = 512
TH_MLP = 512
MIX_BLOCKS = 4
CA = D_A // MIX_BLOCKS
CB = D_B // MIX_BLOCKS
TT = 256
VMEM_LIMIT = 56 * 1024 * 1024


def _params(n_axes):
    return pltpu.CompilerParams(
        dimension_semantics=("arbitrary",) * n_axes,
        vmem_limit_bytes=VMEM_LIMIT)


def _rms(x, g):
    var = jnp.mean(x * x, axis=-1, keepdims=True)
    return x * lax.rsqrt(var + EPS) * g


def _log_sigmoid(x):
    return jnp.minimum(x, 0.0) - jnp.log1p(jnp.exp(-jnp.abs(x)))


def _in_proj_kernel(x_ref, g_ref, w_ref, o_ref, u_ref):
    @pl.when(pl.program_id(1) == 0)
    def _():
        u_ref[...] = _rms(x_ref[...], g_ref[...]).astype(BF16)

    o_ref[...] = jnp.dot(u_ref[...], w_ref[...], preferred_element_type=F32)


def _in_proj(x, g, w, layer):
    m = x.shape[0]
    return pl.pallas_call(
        _in_proj_kernel,
        grid=(m // TM, IN_COLS // TN_IN),
        in_specs=[
            pl.BlockSpec((TM, D_MODEL), lambda i, j: (i, 0)),
            pl.BlockSpec((None, 1, D_MODEL), lambda i, j: (layer, 0, 0)),
            pl.BlockSpec((None, D_MODEL, TN_IN), lambda i, j: (layer, 0, j)),
        ],
        out_specs=pl.BlockSpec((TM, TN_IN), lambda i, j: (i, j)),
        out_shape=jax.ShapeDtypeStruct((m, IN_COLS), F32),
        scratch_shapes=[pltpu.VMEM((TM, D_MODEL), BF16)],
        compiler_params=_params(2),
        name="in_proj",
    )(x, g, w)


def _lru_coeffs(xb, wri_ref, b_r, b_i, lam, a_ref, b_ref):
    xb16 = xb.astype(BF16)
    for h in range(CB // HEAD):
        sl = slice(h * HEAD, (h + 1) * HEAD)
        pre = jnp.dot(xb16[:, sl], wri_ref[h], preferred_element_type=F32)
        r = jax.nn.sigmoid(pre[:, :HEAD] + b_r[:, sl])
        i = jax.nn.sigmoid(pre[:, HEAD:] + b_i[:, sl])
        a = jnp.exp(C_LRU * r * _log_sigmoid(lam[:, sl]))
        mult = jnp.sqrt(1.0 - a * a)
        a_ref[:, sl] = a
        b_ref[:, sl] = mult * i * xb[:, sl]


def _scan_groups(a_ref, b_ref, h_ref, h_in, n_groups):
    row = lax.broadcasted_iota(jnp.int32, (SUBLANES, CB), 0)

    def body(g, h_prev):
        r0 = pl.multiple_of(g * SUBLANES, SUBLANES)
        a = a_ref[pl.ds(r0, SUBLANES), :]
        b = b_ref[pl.ds(r0, SUBLANES), :]
        for s in (1, 2, 4):
            keep = row >= s
            a_sh = jnp.where(keep, pltpu.roll(a, s, 0), 1.0)
            b_sh = jnp.where(keep, pltpu.roll(b, s, 0), 0.0)
            b = a * b_sh + b
            a = a * a_sh
        h = a * h_prev + b
        h_ref[pl.ds(r0, SUBLANES), :] = h
        return jnp.broadcast_to(h[SUBLANES - 1:SUBLANES, :], (SUBLANES, CB))

    return lax.fori_loop(0, n_groups, body, h_in)


def _mixer_prompt_kernel(abg_ref, acg_ref, ax_ref, bx_ref, bg_ref,
                         wa_ref, wb_ref, bias_ref, wri_ref, br_ref, bi_ref, lam_ref,
                         ya_ref, yb_ref, na_ref, nb_ref, hl_ref,
                         exta_ref, extb_ref, hc_ref, a_buf, b_buf, h_buf):
    @pl.when(pl.program_id(2) == 0)
    def _():
        exta_ref[0:SUBLANES, :] = jnp.zeros((SUBLANES, CA), F32)
        extb_ref[0:SUBLANES, :] = jnp.zeros((SUBLANES, CB), F32)
        hc_ref[...] = jnp.zeros((SUBLANES, CB), F32)

    z = acg_ref[...] * ax_ref[...]
    exta_ref[SUBLANES:, :] = z
    wa = wa_ref[...]
    conv = (exta_ref[SUBLANES - 2:SUBLANES - 2 + TT, :] * wa[0:1]
            + exta_ref[SUBLANES - 1:SUBLANES - 1 + TT, :] * wa[1:2]
            + z * wa[2:3])
    ya_ref[...] = (abg_ref[...] * conv).astype(BF16)
    tail_a = exta_ref[TT:TT + SUBLANES, :]
    na_ref[...] = tail_a
    exta_ref[0:SUBLANES, :] = tail_a

    bx = bx_ref[...]
    extb_ref[SUBLANES:, :] = bx
    wb = wb_ref[...]
    xb = (extb_ref[SUBLANES - 3:SUBLANES - 3 + TT, :] * wb[0:1]
          + extb_ref[SUBLANES - 2:SUBLANES - 2 + TT, :] * wb[1:2]
          + extb_ref[SUBLANES - 1:SUBLANES - 1 + TT, :] * wb[2:3]
          + bx * wb[3:4]) + bias_ref[...]
    tail_b = extb_ref[TT:TT + SUBLANES, :]
    nb_ref[...] = tail_b
    extb_ref[0:SUBLANES, :] = tail_b

    _lru_coeffs(xb, wri_ref, br_ref[...], bi_ref[...], lam_ref[...], a_buf, b_buf)
    h_out = _scan_groups(a_buf, b_buf, h_buf, hc_ref[...], TT // SUBLANES)
    hc_ref[...] = h_out
    hl_ref[...] = h_out
    yb_ref[...] = (h_buf[...] * jax.nn.gelu(bg_ref[...])).astype(BF16)


def _mixer_prompt(proj, wa, wb, bias, wri, br, bi, lam, layer, batch, seq):
    nt = seq // TT
    rows = lambda c, s, t: s * nt + t
    a_col = lambda off: pl.BlockSpec(
        (TT, CA), lambda c, s, t: (rows(c, s, t), off // CA + c))
    b_col = lambda off: pl.BlockSpec(
        (TT, CB), lambda c, s, t: (rows(c, s, t), off // CB + c))
    vec_b = pl.BlockSpec((None, 1, CB), lambda c, s, t: (layer, 0, c))
    m = batch * seq
    return pl.pallas_call(
        _mixer_prompt_kernel,
        grid=(MIX_BLOCKS, batch, nt),
        in_specs=[
            a_col(OFF_A_BG), a_col(OFF_A_CG), a_col(OFF_A_X),
            b_col(OFF_B_X), b_col(OFF_B_GATE),
            pl.BlockSpec((None, CONV_A_W, CA), lambda c, s, t: (layer, 0, c)),
            pl.BlockSpec((None, CONV_B_W, CB), lambda c, s, t: (layer, 0, c)),
            vec_b,
            pl.BlockSpec((None, CB // HEAD, HEAD, 2 * HEAD),
                         lambda c, s, t: (layer, c, 0, 0)),
            vec_b, vec_b, vec_b,
        ],
        out_specs=[
            pl.BlockSpec((TT, CA), lambda c, s, t: (rows(c, s, t), c)),
            pl.BlockSpec((TT, CB), lambda c, s, t: (rows(c, s, t), c)),
            pl.BlockSpec((None, SUBLANES, CA), lambda c, s, t: (s, 0, c)),
            pl.BlockSpec((None, SUBLANES, CB), lambda c, s, t: (s, 0, c)),
            pl.BlockSpec((None, SUBLANES, CB), lambda c, s, t: (s, 0, c)),
        ],
        out_shape=[
            jax.ShapeDtypeStruct((m, D_A), BF16),
            jax.ShapeDtypeStruct((m, D_B), BF16),
            jax.ShapeDtypeStruct((batch, SUBLANES, D_A), F32),
            jax.ShapeDtypeStruct((batch, SUBLANES, D_B), F32),
            jax.ShapeDtypeStruct((batch, SUBLANES, D_B), F32),
        ],
        scratch_shapes=[
            pltpu.VMEM((SUBLANES + TT, CA), F32),
            pltpu.VMEM((SUBLANES + TT, CB), F32),
            pltpu.VMEM((SUBLANES, CB), F32),
            pltpu.VMEM((TT, CB), F32),
            pltpu.VMEM((TT, CB), F32),
            pltpu.VMEM((TT, CB), F32),
        ],
        compiler_params=_params(3),
        name="mixer_prompt",
    )(proj, proj, proj, proj, proj, wa, wb, bias, wri, br, bi, lam)


def _mixer_sample_kernel(nb, nsteps, abg_ref, acg_ref, ax_ref, bx_ref, bg_ref,
                         sa0_ref, sa1_ref, sb0_ref, sb1_ref, sb2_ref, h0_ref,
                         wa_ref, wb_ref, bias_ref, wri_ref, br_ref, bi_ref, lam_ref,
                         ya_ref, yb_ref, na_ref, nb_ref, hl_ref, a_buf, b_buf):
    slab = lambda t: slice(t * nb, (t + 1) * nb)

    z = acg_ref[...] * ax_ref[...]
    ins = [sa0_ref[...], sa1_ref[...]] + [z[slab(t), :] for t in range(nsteps)]
    wa = wa_ref[...]
    for t in range(nsteps):
        conv = ins[t] * wa[0:1] + ins[t + 1] * wa[1:2] + ins[t + 2] * wa[2:3]
        ya_ref[slab(t), :] = (abg_ref[slab(t), :] * conv).astype(BF16)
    for k in range(CONV_A_W - 1):
        na_ref[k] = ins[nsteps + k]

    bx = bx_ref[...]
    ins = ([sb0_ref[...], sb1_ref[...], sb2_ref[...]]
           + [bx[slab(t), :] for t in range(nsteps)])
    wb = wb_ref[...]
    xb = jnp.concatenate(
        [(ins[t] * wb[0:1] + ins[t + 1] * wb[1:2] + ins[t + 2] * wb[2:3]
          + ins[t + 3] * wb[3:4]) + bias_ref[...] for t in range(nsteps)], axis=0)
    for k in range(CONV_B_W - 1):
        nb_ref[k] = ins[nsteps + k]

    _lru_coeffs(xb, wri_ref, br_ref[...], bi_ref[...], lam_ref[...], a_buf, b_buf)
    h = h0_ref[...]
    for t in range(nsteps):
        h = a_buf[slab(t), :] * h + b_buf[slab(t), :]
        yb_ref[slab(t), :] = (h * jax.nn.gelu(bg_ref[slab(t), :])).astype(BF16)
    hl_ref[...] = h


def _mixer_sample(proj, sa, sb, h0, wa, wb, bias, wri, br, bi, lam, layer, nb, nsteps):
    m = nb * nsteps
    a_col = lambda off: pl.BlockSpec((m, CA), lambda c: (0, off // CA + c))
    b_col = lambda off: pl.BlockSpec((m, CB), lambda c: (0, off // CB + c))
    vec_b = pl.BlockSpec((None, 1, CB), lambda c: (layer, 0, c))
    st_a = lambda k: pl.BlockSpec((None, nb, CA), lambda c: (layer, 0, k * MIX_BLOCKS + c))
    st_b = lambda k: pl.BlockSpec((None, nb, CB), lambda c: (layer, 0, k * MIX_BLOCKS + c))
    return pl.pallas_call(
        functools.partial(_mixer_sample_kernel, nb, nsteps),
        grid=(MIX_BLOCKS,),
        in_specs=[
            a_col(OFF_A_BG), a_col(OFF_A_CG), a_col(OFF_A_X),
            b_col(OFF_B_X), b_col(OFF_B_GATE),
            st_a(0), st_a(1), st_b(0), st_b(1), st_b(2),
            pl.BlockSpec((None, nb, CB), lambda c: (layer, 0, c)),
            pl.BlockSpec((None, CONV_A_W, CA), lambda c: (layer, 0, c)),
            pl.BlockSpec((None, CONV_B_W, CB), lambda c: (layer, 0, c)),
            vec_b,
            pl.BlockSpec((None, CB // HEAD, HEAD, 2 * HEAD), lambda c: (layer, c, 0, 0)),
            vec_b, vec_b, vec_b,
        ],
        out_specs=[
            pl.BlockSpec((m, CA), lambda c: (0, c)),
            pl.BlockSpec((m, CB), lambda c: (0, c)),
            pl.BlockSpec((CONV_A_W - 1, nb, CA), lambda c: (0, 0, c)),
            pl.BlockSpec((CONV_B_W - 1, nb, CB), lambda c: (0, 0, c)),
            pl.BlockSpec((nb, CB), lambda c: (0, c)),
        ],
        out_shape=[
            jax.ShapeDtypeStruct((m, D_A), BF16),
            jax.ShapeDtypeStruct((m, D_B), BF16),
            jax.ShapeDtypeStruct((CONV_A_W - 1, nb, D_A), F32),
            jax.ShapeDtypeStruct((CONV_B_W - 1, nb, D_B), F32),
            jax.ShapeDtypeStruct((nb, D_B), F32),
        ],
        scratch_shapes=[pltpu.VMEM((m, CB), F32), pltpu.VMEM((m, CB), F32)],
        compiler_params=_params(1),
        name="mixer_sample",
    )(proj, proj, proj, proj, proj, sa, sa, sb, sb, sb, h0,
      wa, wb, bias, wri, br, bi, lam)


def _merge_kernel(ya_ref, yb_ref, ga_ref, gb_ref, woa_ref, wob_ref, wo_ref,
                  x_ref, g_ref, o_ref, acc_ref):
    j = pl.program_id(1)

    @pl.when(j == 0)
    def _():
        acc_ref[...] = jnp.zeros_like(acc_ref)

    y_a = jnp.dot(ya_ref[...], woa_ref[...], preferred_element_type=F32)
    y_b = jnp.dot(yb_ref[...], wob_ref[...], preferred_element_type=F32)
    merged = jax.nn.sigmoid(ga_ref[...]) * y_a + jax.nn.sigmoid(gb_ref[...]) * y_b
    acc_ref[...] += jnp.dot(merged.astype(BF16), wo_ref[...], preferred_element_type=F32)

    @pl.when(j == pl.num_programs(1) - 1)
    def _():
        o_ref[...] = x_ref[...] + _rms(acc_ref[...], g_ref[...])


def _merge(ya, yb, proj, woa, wob, wo, x, g, layer):
    m = x.shape[0]
    tc = TC_MERGE
    return pl.pallas_call(
        _merge_kernel,
        grid=(m // TM, D_MODEL // tc),
        in_specs=[
            pl.BlockSpec((TM, D_A), lambda i, j: (i, 0)),
            pl.BlockSpec((TM, D_B), lambda i, j: (i, 0)),
            pl.BlockSpec((TM, tc), lambda i, j: (i, OFF_GATE_A // tc + j)),
            pl.BlockSpec((TM, tc), lambda i, j: (i, OFF_GATE_B // tc + j)),
            pl.BlockSpec((None, D_A, tc), lambda i, j: (layer, 0, j)),
            pl.BlockSpec((None, D_B, tc), lambda i, j: (layer, 0, j)),
            pl.BlockSpec((None, tc, D_MODEL), lambda i, j: (layer, j, 0)),
            pl.BlockSpec((TM, D_MODEL), lambda i, j: (i, 0)),
            pl.BlockSpec((None, 1, D_MODEL), lambda i, j: (layer, 0, 0)),
        ],
        out_specs=pl.BlockSpec((TM, D_MODEL), lambda i, j: (i, 0)),
        out_shape=jax.ShapeDtypeStruct((m, D_MODEL), F32),
        scratch_shapes=[pltpu.VMEM((TM, D_MODEL), F32)],
        compiler_params=_params(2),
        name="merge",
    )(ya, yb, proj, proj, woa, wob, wo, x, g)


def _mlp_kernel(x_ref, gpre_ref, wg_ref, wu_ref, wd_ref, gpost_ref, o_ref,
                v_ref, acc_ref):
    j = pl.program_id(1)

    @pl.when(j == 0)
    def _():
        v_ref[...] = _rms(x_ref[...], gpre_ref[...]).astype(BF16)
        acc_ref[...] = jnp.zeros_like(acc_ref)

    v = v_ref[...]
    g = jnp.dot(v, wg_ref[...], preferred_element_type=F32)
    u = jnp.dot(v, wu_ref[...], preferred_element_type=F32)
    hidden = (g * jax.nn.sigmoid(g)) * u
    acc_ref[...] += jnp.dot(hidden.astype(BF16), wd_ref[...], preferred_element_type=F32)

    @pl.when(j == pl.num_programs(1) - 1)
    def _():
        o_ref[...] = x_ref[...] + _rms(acc_ref[...], gpost_ref[...])


def _mlp(x, gpre, wgu, wd, gpost, layer):
    m = x.shape[0]
    nh = FFN_HIDDEN // TH_MLP
    return pl.pallas_call(
        _mlp_kernel,
        grid=(m // TM, nh),
        in_specs=[
            pl.BlockSpec((TM, D_MODEL), lambda i, j: (i, 0)),
            pl.BlockSpec((None, 1, D_MODEL), lambda i, j: (layer, 0, 0)),
            pl.BlockSpec((None, D_MODEL, TH_MLP), lambda i, j: (layer, 0, j)),
            pl.BlockSpec((None, D_MODEL, TH_MLP), lambda i, j: (layer, 0, nh + j)),
            pl.BlockSpec((None, TH_MLP, D_MODEL), lambda i, j: (layer, j, 0)),
            pl.BlockSpec((None, 1, D_MODEL), lambda i, j: (layer, 0, 0)),
        ],
        out_specs=pl.BlockSpec((TM, D_MODEL), lambda i, j: (i, 0)),
        out_shape=jax.ShapeDtypeStruct((m, D_MODEL), F32),
        scratch_shapes=[pltpu.VMEM((TM, D_MODEL), BF16), pltpu.VMEM((TM, D_MODEL), F32)],
        compiler_params=_params(2),
        name="mlp",
    )(x, gpre, wgu, wgu, wd, gpost)


def kernel(x_prompt, x_sample, state_conv_a, state_conv_b, state_lru_h, w_in, conv_a_w,
           w_out_a, conv_b_w, conv_b_bias, w_r, b_r, w_i, b_i, lru_lambda, w_out_b, w_o,
           norm_pre_mix, norm_post_mix, norm_pre_ffn, norm_post_ffn, w_gate_up, w_down):
    batch, seq, _ = x_prompt.shape
    nb, nsteps, _ = x_sample.shape
    depth = w_in.shape[0]
    assert seq % TT == 0 and (batch * seq) % TM == 0 and (nb * nsteps) % TM == 0
    assert nsteps >= CONV_B_W - 1

    w_in16, w_out_a16, w_out_b16 = w_in.astype(BF16), w_out_a.astype(BF16), w_out_b.astype(BF16)
    w_o16, w_gu16, w_down16 = w_o.astype(BF16), w_gate_up.astype(BF16), w_down.astype(BF16)
    wri16 = jnp.concatenate([w_r, w_i], axis=-1).astype(BF16)
    row = lambda v: v.reshape(depth, 1, v.shape[-1])
    bias, br, bi, lam = row(conv_b_bias), row(b_r), row(b_i), row(lru_lambda)
    g_pre_mix, g_post_mix = row(norm_pre_mix), row(norm_post_mix)
    g_pre_ffn, g_post_ffn = row(norm_pre_ffn), row(norm_post_ffn)
    sa = state_conv_a.reshape(depth, nb, (CONV_A_W - 1) * D_A)
    sb = state_conv_b.reshape(depth, nb, (CONV_B_W - 1) * D_B)

    xp = x_prompt.reshape(batch * seq, D_MODEL)
    xs = x_sample.transpose(1, 0, 2).reshape(nsteps * nb, D_MODEL)

    pa, pb, ph, s_a, s_b, s_h = [], [], [], [], [], []
    for l in range(depth):
        proj_p = _in_proj(xp, g_pre_mix, w_in16, l)
        proj_s = _in_proj(xs, g_pre_mix, w_in16, l)
        ya_p, yb_p, na_p, nb_p, hl_p = _mixer_prompt(
            proj_p, conv_a_w, conv_b_w, bias, wri16, br, bi, lam, l, batch, seq)
        ya_s, yb_s, na_s, nb_s, hl_s = _mixer_sample(
            proj_s, sa, sb, state_lru_h, conv_a_w, conv_b_w, bias, wri16, br, bi, lam,
            l, nb, nsteps)
        xp = _merge(ya_p, yb_p, proj_p, w_out_a16, w_out_b16, w_o16, xp, g_post_mix, l)
        xs = _merge(ya_s, yb_s, proj_s, w_out_a16, w_out_b16, w_o16, xs, g_post_mix, l)
        xp = _mlp(xp, g_pre_ffn, w_gu16, w_down16, g_post_ffn, l)
        xs = _mlp(xs, g_pre_ffn, w_gu16, w_down16, g_post_ffn, l)
        pa.append(na_p[:, SUBLANES - (CONV_A_W - 1):, :])
        pb.append(nb_p[:, SUBLANES - (CONV_B_W - 1):, :])
        ph.append(hl_p[:, 0, :])
        s_a.append(na_s)
        s_b.append(nb_s)
        s_h.append(hl_s)

    y_prompt = xp.reshape(batch, seq, D_MODEL)
    y_sample = xs.reshape(nsteps, nb, D_MODEL).transpose(1, 0, 2)
    return (y_prompt, y_sample, jnp.stack(pa), jnp.stack(pb), jnp.stack(ph),
            jnp.stack(s_a).transpose(0, 2, 1, 3), jnp.stack(s_b).transpose(0, 2, 1, 3),
            jnp.stack(s_h))
```

```python
import functools

import jax
import jax.numpy as jnp
from jax import lax
from jax.experimental import pallas as pl
from jax.experimental.pallas import tpu as pltpu

F32 = jnp.float32
BF16 = jnp.bfloat16

D_MODEL = 2048
D_A = D_MODEL // 2
D_B = D_MODEL
CONV_A_W = 3
CONV_B_W = 4
HEAD = 128
C_LRU = 8.0
LOG2_E = 1.4426950408889634
FFN_HIDDEN = 5632
EPS = 1e-6
IN_COLS = 3 * D_A + 2 * D_B + 2 * D_MODEL

OFF_A_BG, OFF_A_CG, OFF_A_X = 0, D_A, 2 * D_A
OFF_B_X = 3 * D_A
OFF_B_GATE = OFF_B_X + D_B
OFF_GATE_A = OFF_B_GATE + D_B
OFF_GATE_B = OFF_GATE_A + D_MODEL

SUBLANES = 8
TM = 512
TC_MERGE = 512
TH_MLP = 512
MIX_BLOCKS = 4
CA = D_A // MIX_BLOCKS
CB = D_B // MIX_BLOCKS
VMEM_LIMIT = 56 * 1024 * 1024


def _params(n_axes):
    return pltpu.CompilerParams(
        dimension_semantics=("arbitrary",) * n_axes,
        vmem_limit_bytes=VMEM_LIMIT)


def _rms(x, g):
    var = jnp.mean(x * x, axis=-1, keepdims=True)
    return x * lax.rsqrt(var + EPS) * g


def _log_sigmoid(x):
    return jnp.minimum(x, 0.0) - jnp.log1p(jnp.exp(-jnp.abs(x)))


def _head(h):
    return slice(h * HEAD, (h + 1) * HEAD)


def _gate_pre(xb, wri_ref, h):
    return jnp.dot(xb[:, _head(h)].astype(BF16), wri_ref[h], preferred_element_type=F32)


def _lru_coeffs_head(xb, pre, b_r, b_i, lam, a_ref, b_ref, h):
    sl = _head(h)
    r = jax.nn.sigmoid(pre[:, :HEAD] + b_r[:, sl])
    i = jax.nn.sigmoid(pre[:, HEAD:] + b_i[:, sl])
    a = jnp.exp2(r * ((C_LRU * LOG2_E) * _log_sigmoid(lam[:, sl])))
    y = 1.0 - a * a
    mult = jnp.where(y > 0.0, y * lax.rsqrt(y), 0.0)
    a_ref[:, sl] = a
    b_ref[:, sl] = mult * i * xb[:, sl]


def _lru_coeffs(xb, wri_ref, b_r, b_i, lam, a_ref, b_ref):
    for h in range(CB // HEAD):
        _lru_coeffs_head(xb, _gate_pre(xb, wri_ref, h), b_r, b_i, lam, a_ref, b_ref, h)


def _scan_groups(a_ref, b_ref, h_ref, h_in, n_groups):
    row = lax.broadcasted_iota(jnp.int32, (SUBLANES, CB), 0)

    h_prev = h_in
    for g in range(n_groups):
        rows = slice(g * SUBLANES, (g + 1) * SUBLANES)
        a = a_ref[rows, :]
        b = b_ref[rows, :]
        for s in (1, 2, 4):
            keep = row >= s
            a_sh = jnp.where(keep, pltpu.roll(a, s, 0), 1.0)
            b_sh = jnp.where(keep, pltpu.roll(b, s, 0), 0.0)
            b = a * b_sh + b
            a = a * a_sh
        h = a * h_prev + b
        h_ref[rows, :] = h
        h_prev = jnp.broadcast_to(h[SUBLANES - 1:SUBLANES, :], (SUBLANES, CB))
    return h_prev


def _norm(x_ref, g_ref, u_ref):
    @pl.when(pl.program_id(1) == 0)
    def _():
        u_ref[...] = _rms(x_ref[...], g_ref[...]).astype(BF16)

    return u_ref[...]


def _gates(u, wga_ref, wgb_ref, sga_ref, sgb_ref):
    sga_ref[...] = jax.nn.sigmoid(
        jnp.dot(u, wga_ref[...], preferred_element_type=F32)).astype(BF16)
    sgb_ref[...] = jax.nn.sigmoid(
        jnp.dot(u, wgb_ref[...], preferred_element_type=F32)).astype(BF16)


def _proj_mix_prompt_kernel(tiles_per_seq, x_ref, g_ref,
                            wabg_ref, wacg_ref, wax_ref, wbx_ref, wbg_ref, wga_ref, wgb_ref,
                            wa_ref, wb_ref, bias_ref, wri_ref, br_ref, bi_ref, lam_ref,
                            ya_ref, yb_ref, sga_ref, sgb_ref, na_ref, nb_ref, hl_ref,
                            u_ref, exta_ref, extb_ref, taila_ref, tailb_ref, hc_ref,
                            a_buf, b_buf, h_buf):
    c = pl.program_id(1)
    u = _norm(x_ref, g_ref, u_ref)
    dot = lambda w_ref: jnp.dot(u, w_ref[...], preferred_element_type=F32)

    @pl.when(pl.program_id(0) % tiles_per_seq == 0)
    def _():
        taila_ref[c] = jnp.zeros((SUBLANES, CA), F32)
        tailb_ref[c] = jnp.zeros((SUBLANES, CB), F32)
        hc_ref[c] = jnp.zeros((SUBLANES, CB), F32)

    pre_ga = dot(wga_ref)
    bx = dot(wbx_ref)
    sga_ref[...] = jax.nn.sigmoid(pre_ga).astype(BF16)

    pre_gb = dot(wgb_ref)
    extb_ref[0:SUBLANES, :] = tailb_ref[c]
    extb_ref[SUBLANES:, :] = bx
    wb = wb_ref[...]
    xb = (extb_ref[SUBLANES - 3:SUBLANES - 3 + TM, :] * wb[0:1]
          + extb_ref[SUBLANES - 2:SUBLANES - 2 + TM, :] * wb[1:2]
          + extb_ref[SUBLANES - 1:SUBLANES - 1 + TM, :] * wb[2:3]
          + bx * wb[3:4]) + bias_ref[...]
    tail_b = extb_ref[TM:TM + SUBLANES, :]
    nb_ref[...] = tail_b
    tailb_ref[c] = tail_b

    heads = range(CB // HEAD)
    pres = [_gate_pre(xb, wri_ref, h) for h in heads]
    acg = dot(wacg_ref)
    sgb_ref[...] = jax.nn.sigmoid(pre_gb).astype(BF16)
    coeffs = functools.partial(_lru_coeffs_head, xb, b_r=br_ref[...], b_i=bi_ref[...],
                               lam=lam_ref[...], a_ref=a_buf, b_ref=b_buf)
    ax = dot(wax_ref)
    coeffs(pre=pres[0], h=0)
    bg = dot(wbg_ref)
    coeffs(pre=pres[1], h=1)
    coeffs(pre=pres[2], h=2)

    z = acg * ax
    exta_ref[0:SUBLANES, :] = taila_ref[c]
    exta_ref[SUBLANES:, :] = z
    wa = wa_ref[...]
    conv = (exta_ref[SUBLANES - 2:SUBLANES - 2 + TM, :] * wa[0:1]
            + exta_ref[SUBLANES - 1:SUBLANES - 1 + TM, :] * wa[1:2]
            + z * wa[2:3])
    tail_a = exta_ref[TM:TM + SUBLANES, :]
    na_ref[...] = tail_a
    taila_ref[c] = tail_a

    coeffs(pre=pres[3], h=3)
    abg = dot(wabg_ref)
    h_out = _scan_groups(a_buf, b_buf, h_buf, hc_ref[c], TM // SUBLANES)
    hc_ref[c] = h_out
    hl_ref[...] = h_out
    yb_ref[...] = (h_buf[...] * jax.nn.gelu(bg)).astype(BF16)
    ya_ref[...] = (abg * conv).astype(BF16)


def _proj_mix_sample_kernel(nb, nsteps, x_ref, g_ref,
                            wabg_ref, wacg_ref, wax_ref, wbx_ref, wbg_ref, wga_ref, wgb_ref,
                            wa_ref, wb_ref, bias_ref, wri_ref, br_ref, bi_ref, lam_ref,
                            sa0_ref, sa1_ref, sb0_ref, sb1_ref, sb2_ref, h0_ref,
                            ya_ref, yb_ref, sga_ref, sgb_ref, na_ref, nb_ref, hl_ref,
                            u_ref, a_buf, b_buf):
    u = _norm(x_ref, g_ref, u_ref)
    _gates(u, wga_ref, wgb_ref, sga_ref, sgb_ref)
    dot = lambda w_ref: jnp.dot(u, w_ref[...], preferred_element_type=F32)
    slab = lambda t: slice(t * nb, (t + 1) * nb)

    z = dot(wacg_ref) * dot(wax_ref)
    abg = dot(wabg_ref)
    ins = [sa0_ref[...], sa1_ref[...]] + [z[slab(t), :] for t in range(nsteps)]
    wa = wa_ref[...]
    for t in range(nsteps):
        conv = ins[t] * wa[0:1] + ins[t + 1] * wa[1:2] + ins[t + 2] * wa[2:3]
        ya_ref[slab(t), :] = (abg[slab(t), :] * conv).astype(BF16)
    for k in range(CONV_A_W - 1):
        na_ref[k] = ins[nsteps + k]

    bx = dot(wbx_ref)
    ins = ([sb0_ref[...], sb1_ref[...], sb2_ref[...]]
           + [bx[slab(t), :] for t in range(nsteps)])
    wb = wb_ref[...]
    xb = jnp.concatenate(
        [(ins[t] * wb[0:1] + ins[t + 1] * wb[1:2] + ins[t + 2] * wb[2:3]
          + ins[t + 3] * wb[3:4]) + bias_ref[...] for t in range(nsteps)], axis=0)
    for k in range(CONV_B_W - 1):
        nb_ref[k] = ins[nsteps + k]

    _lru_coeffs(xb, wri_ref, br_ref[...], bi_ref[...], lam_ref[...], a_buf, b_buf)
    gate = jax.nn.gelu(dot(wbg_ref))
    h = h0_ref[...]
    for t in range(nsteps):
        h = a_buf[slab(t), :] * h + b_buf[slab(t), :]
        yb_ref[slab(t), :] = (h * gate[slab(t), :]).astype(BF16)
    hl_ref[...] = h


def _proj_mix(x, g, w_in16, wa, wb, bias, wri, br, bi, lam, layer, *,
              seq=None, states=None):
    m = x.shape[0]
    w_col = lambda width, off: pl.BlockSpec(
        (None, D_MODEL, width), lambda i, c: (layer, 0, off // width + c))
    vec_b = pl.BlockSpec((None, 1, CB), lambda i, c: (layer, 0, c))
    in_specs = [
        pl.BlockSpec((TM, D_MODEL), lambda i, c: (i, 0)),
        pl.BlockSpec((None, 1, D_MODEL), lambda i, c: (layer, 0, 0)),
        w_col(CA, OFF_A_BG), w_col(CA, OFF_A_CG), w_col(CA, OFF_A_X),
        w_col(CB, OFF_B_X), w_col(CB, OFF_B_GATE),
        w_col(CB, OFF_GATE_A), w_col(CB, OFF_GATE_B),
        pl.BlockSpec((None, CONV_A_W, CA), lambda i, c: (layer, 0, c)),
        pl.BlockSpec((None, CONV_B_W, CB), lambda i, c: (layer, 0, c)),
        vec_b,
        pl.BlockSpec((None, CB // HEAD, HEAD, 2 * HEAD), lambda i, c: (layer, c, 0, 0)),
        vec_b, vec_b, vec_b,
    ]
    args = [x, g] + [w_in16] * 7 + [wa, wb, bias, wri, br, bi, lam]
    out_specs = [
        pl.BlockSpec((TM, CA), lambda i, c: (i, c)),
        pl.BlockSpec((TM, CB), lambda i, c: (i, c)),
        pl.BlockSpec((TM, CB), lambda i, c: (i, c)),
        pl.BlockSpec((TM, CB), lambda i, c: (i, c)),
    ]
    out_shape = [
        jax.ShapeDtypeStruct((m, D_A), BF16),
        jax.ShapeDtypeStruct((m, D_B), BF16),
        jax.ShapeDtypeStruct((m, D_MODEL), BF16),
        jax.ShapeDtypeStruct((m, D_MODEL), BF16),
    ]
    scratch = [pltpu.VMEM((TM, D_MODEL), BF16)]
    if states is None:
        tiles_per_seq = seq // TM
        body = functools.partial(_proj_mix_prompt_kernel, tiles_per_seq)
        out_specs += [
            pl.BlockSpec((None, SUBLANES, CA), lambda i, c: (i, 0, c)),
            pl.BlockSpec((None, SUBLANES, CB), lambda i, c: (i, 0, c)),
            pl.BlockSpec((None, SUBLANES, CB), lambda i, c: (i, 0, c)),
        ]
        out_shape += [
            jax.ShapeDtypeStruct((m // TM, SUBLANES, D_A), F32),
            jax.ShapeDtypeStruct((m // TM, SUBLANES, D_B), F32),
            jax.ShapeDtypeStruct((m // TM, SUBLANES, D_B), F32),
        ]
        scratch += [
            pltpu.VMEM((SUBLANES + TM, CA), F32),
            pltpu.VMEM((SUBLANES + TM, CB), F32),
            pltpu.VMEM((MIX_BLOCKS, SUBLANES, CA), F32),
            pltpu.VMEM((MIX_BLOCKS, SUBLANES, CB), F32),
            pltpu.VMEM((MIX_BLOCKS, SUBLANES, CB), F32),
            pltpu.VMEM((TM, CB), F32),
            pltpu.VMEM((TM, CB), F32),
            pltpu.VMEM((TM, CB), F32),
        ]
        name = "proj_mix_prompt"
    else:
        sa, sb, h0, nb, nsteps = states
        assert m == TM == nb * nsteps
        body = functools.partial(_proj_mix_sample_kernel, nb, nsteps)
        st_a = lambda k: pl.BlockSpec(
            (None, nb, CA), lambda i, c: (layer, 0, k * MIX_BLOCKS + c))
        st_b = lambda k: pl.BlockSpec(
            (None, nb, CB), lambda i, c: (layer, 0, k * MIX_BLOCKS + c))
        in_specs += [st_a(0), st_a(1), st_b(0), st_b(1), st_b(2),
                     pl.BlockSpec((None, nb, CB), lambda i, c: (layer, 0, c))]
        args += [sa, sa, sb, sb, sb, h0]
        out_specs += [
            pl.BlockSpec((CONV_A_W - 1, nb, CA), lambda i, c: (0, 0, c)),
            pl.BlockSpec((CONV_B_W - 1, nb, CB), lambda i, c: (0, 0, c)),
            pl.BlockSpec((nb, CB), lambda i, c: (0, c)),
        ]
        out_shape += [
            jax.ShapeDtypeStruct((CONV_A_W - 1, nb, D_A), F32),
            jax.ShapeDtypeStruct((CONV_B_W - 1, nb, D_B), F32),
            jax.ShapeDtypeStruct((nb, D_B), F32),
        ]
        scratch += [pltpu.VMEM((TM, CB), F32), pltpu.VMEM((TM, CB), F32)]
        name = "proj_mix_sample"
    return pl.pallas_call(
        body,
        grid=(m // TM, MIX_BLOCKS),
        in_specs=in_specs,
        out_specs=out_specs,
        out_shape=out_shape,
        scratch_shapes=scratch,
        compiler_params=_params(2),
        name=name,
    )(*args)


def _merge_kernel(ya_ref, yb_ref, sga_ref, sgb_ref, woa_ref, wob_ref, wo_ref,
                  x_ref, g_ref, o_ref, acc_ref):
    j = pl.program_id(1)

    @pl.when(j == 0)
    def _():
        acc_ref[...] = jnp.zeros_like(acc_ref)

    y_a = jnp.dot(ya_ref[...], woa_ref[...], preferred_element_type=F32)
    y_b = jnp.dot(yb_ref[...], wob_ref[...], preferred_element_type=F32)
    merged = sga_ref[...].astype(F32) * y_a + sgb_ref[...].astype(F32) * y_b
    acc_ref[...] += jnp.dot(merged.astype(BF16), wo_ref[...], preferred_element_type=F32)

    @pl.when(j == pl.num_programs(1) - 1)
    def _():
        o_ref[...] = x_ref[...] + _rms(acc_ref[...], g_ref[...])


def _merge(ya, yb, sga, sgb, woa, wob, wo, x, g, layer):
    m = x.shape[0]
    tc = TC_MERGE
    return pl.pallas_call(
        _merge_kernel,
        grid=(m // TM, D_MODEL // tc),
        in_specs=[
            pl.BlockSpec((TM, D_A), lambda i, j: (i, 0)),
            pl.BlockSpec((TM, D_B), lambda i, j: (i, 0)),
            pl.BlockSpec((TM, tc), lambda i, j: (i, j)),
            pl.BlockSpec((TM, tc), lambda i, j: (i, j)),
            pl.BlockSpec((None, D_A, tc), lambda i, j: (layer, 0, j)),
            pl.BlockSpec((None, D_B, tc), lambda i, j: (layer, 0, j)),
            pl.BlockSpec((None, tc, D_MODEL), lambda i, j: (layer, j, 0)),
            pl.BlockSpec((TM, D_MODEL), lambda i, j: (i, 0)),
            pl.BlockSpec((None, 1, D_MODEL), lambda i, j: (layer, 0, 0)),
        ],
        out_specs=pl.BlockSpec((TM, D_MODEL), lambda i, j: (i, 0)),
        out_shape=jax.ShapeDtypeStruct((m, D_MODEL), F32),
        scratch_shapes=[pltpu.VMEM((TM, D_MODEL), F32)],
        compiler_params=_params(2),
        name="merge",
    )(ya, yb, sga, sgb, woa, wob, wo, x, g)


def _mlp_kernel(x_ref, gpre_ref, wg_ref, wu_ref, wd_ref, gpost_ref, o_ref,
                v_ref, acc_ref):
    j = pl.program_id(1)

    @pl.when(j == 0)
    def _():
        v_ref[...] = _rms(x_ref[...], gpre_ref[...]).astype(BF16)
        acc_ref[...] = jnp.zeros_like(acc_ref)

    v = v_ref[...]
    g = jnp.dot(v, wg_ref[...], preferred_element_type=F32)
    u = jnp.dot(v, wu_ref[...], preferred_element_type=F32)
    hidden = (g * jax.nn.sigmoid(g)) * u
    acc_ref[...] += jnp.dot(hidden.astype(BF16), wd_ref[...], preferred_element_type=F32)

    @pl.when(j == pl.num_programs(1) - 1)
    def _():
        o_ref[...] = x_ref[...] + _rms(acc_ref[...], gpost_ref[...])


def _mlp(x, gpre, wgu, wd, gpost, layer):
    m = x.shape[0]
    nh = FFN_HIDDEN // TH_MLP
    return pl.pallas_call(
        _mlp_kernel,
        grid=(m // TM, nh),
        in_specs=[
            pl.BlockSpec((TM, D_MODEL), lambda i, j: (i, 0)),
            pl.BlockSpec((None, 1, D_MODEL), lambda i, j: (layer, 0, 0)),
            pl.BlockSpec((None, D_MODEL, TH_MLP), lambda i, j: (layer, 0, j)),
            pl.BlockSpec((None, D_MODEL, TH_MLP), lambda i, j: (layer, 0, nh + j)),
            pl.BlockSpec((None, TH_MLP, D_MODEL), lambda i, j: (layer, j, 0)),
            pl.BlockSpec((None, 1, D_MODEL), lambda i, j: (layer, 0, 0)),
        ],
        out_specs=pl.BlockSpec((TM, D_MODEL), lambda i, j: (i, 0)),
        out_shape=jax.ShapeDtypeStruct((m, D_MODEL), F32),
        scratch_shapes=[pltpu.VMEM((TM, D_MODEL), BF16), pltpu.VMEM((TM, D_MODEL), F32)],
        compiler_params=_params(2),
        name="mlp",
    )(x, gpre, wgu, wgu, wd, gpost)


def kernel(x_prompt, x_sample, state_conv_a, state_conv_b, state_lru_h, w_in, conv_a_w,
           w_out_a, conv_b_w, conv_b_bias, w_r, b_r, w_i, b_i, lru_lambda, w_out_b, w_o,
           norm_pre_mix, norm_post_mix, norm_pre_ffn, norm_post_ffn, w_gate_up, w_down):
    batch, seq, _ = x_prompt.shape
    nb, nsteps, _ = x_sample.shape
    depth = w_in.shape[0]
    assert seq % TM == 0 and nb * nsteps == TM
    assert nsteps >= CONV_B_W - 1

    w_in16, w_out_a16, w_out_b16 = w_in.astype(BF16), w_out_a.astype(BF16), w_out_b.astype(BF16)
    w_o16, w_gu16, w_down16 = w_o.astype(BF16), w_gate_up.astype(BF16), w_down.astype(BF16)
    wri16 = jnp.concatenate([w_r, w_i], axis=-1).astype(BF16)
    row = lambda v: v.reshape(depth, 1, v.shape[-1])
    bias, br, bi, lam = row(conv_b_bias), row(b_r), row(b_i), row(lru_lambda)
    g_pre_mix, g_post_mix = row(norm_pre_mix), row(norm_post_mix)
    g_pre_ffn, g_post_ffn = row(norm_pre_ffn), row(norm_post_ffn)
    sa = state_conv_a.reshape(depth, nb, (CONV_A_W - 1) * D_A)
    sb = state_conv_b.reshape(depth, nb, (CONV_B_W - 1) * D_B)

    xp = x_prompt.reshape(batch * seq, D_MODEL)
    xs = x_sample.transpose(1, 0, 2).reshape(nsteps * nb, D_MODEL)

    pa, pb, ph, s_a, s_b, s_h = [], [], [], [], [], []
    for l in range(depth):
        mix_w = (w_in16, conv_a_w, conv_b_w, bias, wri16, br, bi, lam, l)
        ya_p, yb_p, sga_p, sgb_p, na_p, nb_p, hl_p = _proj_mix(
            xp, g_pre_mix, *mix_w, seq=seq)
        ya_s, yb_s, sga_s, sgb_s, na_s, nb_s, hl_s = _proj_mix(
            xs, g_pre_mix, *mix_w, states=(sa, sb, state_lru_h, nb, nsteps))
        out_w = (w_out_a16, w_out_b16, w_o16)
        xp = _merge(ya_p, yb_p, sga_p, sgb_p, *out_w, xp, g_post_mix, l)
        xs = _merge(ya_s, yb_s, sga_s, sgb_s, *out_w, xs, g_post_mix, l)
        xp = _mlp(xp, g_pre_ffn, w_gu16, w_down16, g_post_ffn, l)
        xs = _mlp(xs, g_pre_ffn, w_gu16, w_down16, g_post_ffn, l)
        last = slice(seq // TM - 1, None, seq // TM)
        pa.append(na_p[last, SUBLANES - (CONV_A_W - 1):, :])
        pb.append(nb_p[last, SUBLANES - (CONV_B_W - 1):, :])
        ph.append(hl_p[last, 0, :])
        s_a.append(na_s)
        s_b.append(nb_s)
        s_h.append(hl_s)

    y_prompt = xp.reshape(batch, seq, D_MODEL)
    y_sample = xs.reshape(nsteps, nb, D_MODEL).transpose(1, 0, 2)
    return (y_prompt, y_sample, jnp.stack(pa), jnp.stack(pb), jnp.stack(ph),
            jnp.stack(s_a).transpose(0, 2, 1, 3), jnp.stack(s_b).transpose(0, 2, 1, 3),
            jnp.stack(s_h))
```

```python
import functools

import jax
import jax.numpy as jnp
from jax import lax
from jax.experimental import pallas as pl
from jax.experimental.pallas import tpu as pltpu

F32 = jnp.float32
BF16 = jnp.bfloat16

D_MODEL = 2048
D_A = D_MODEL // 2
D_B = D_MODEL
CONV_A_W = 3
CONV_B_W = 4
HEAD = 128
C_LRU = 8.0
LOG2_E = 1.4426950408889634
FFN_HIDDEN = 5632
EPS = 1e-6
IN_COLS = 3 * D_A + 2 * D_B + 2 * D_MODEL

OFF_A_BG, OFF_A_CG, OFF_A_X = 0, D_A, 2 * D_A
OFF_B_X = 3 * D_A
OFF_B_GATE = OFF_B_X + D_B
OFF_GATE_A = OFF_B_GATE + D_B
OFF_GATE_B = OFF_GATE_A + D_MODEL

SUBLANES = 8
BF16_ROWS = 16
LANES = 128
TM = 512
TC_MERGE = 512
TH_MLP = 512
SQUARE_CAST_COLS = 256
MIX_BLOCKS = 4
CA = D_A // MIX_BLOCKS
CB = D_B // MIX_BLOCKS
VMEM_LIMIT = 56 * 1024 * 1024


def _params(n_axes):
    return pltpu.CompilerParams(
        dimension_semantics=("arbitrary",) * n_axes,
        vmem_limit_bytes=VMEM_LIMIT)


def _rms(x, g):
    var = jnp.mean(x * x, axis=-1, keepdims=True)
    return x * lax.rsqrt(var + EPS) * g


def _log_sigmoid(x):
    return jnp.minimum(x, 0.0) - jnp.log1p(jnp.exp(-jnp.abs(x)))


def _head(h):
    return slice(h * HEAD, (h + 1) * HEAD)


def _gate_pre(xb, wri_ref, h):
    return jnp.dot(xb[:, _head(h)].astype(BF16), wri_ref[h], preferred_element_type=F32)


def _lru_coeffs_head(xb, pre, b_r, b_i, lam, a_ref, b_ref, h):
    sl = _head(h)
    r = jax.nn.sigmoid(pre[:, :HEAD] + b_r[:, sl])
    i = jax.nn.sigmoid(pre[:, HEAD:] + b_i[:, sl])
    a = jnp.exp2(r * ((C_LRU * LOG2_E) * _log_sigmoid(lam[:, sl])))
    y = 1.0 - a * a
    mult = jnp.where(y > 0.0, y * lax.rsqrt(y), 0.0)
    a_ref[:, sl] = a
    b_ref[:, sl] = mult * i * xb[:, sl]


def _lru_coeffs(xb, wri_ref, b_r, b_i, lam, a_ref, b_ref):
    for h in range(CB // HEAD):
        _lru_coeffs_head(xb, _gate_pre(xb, wri_ref, h), b_r, b_i, lam, a_ref, b_ref, h)


def _scan_groups(a_ref, b_ref, h_ref, h_in, n_groups):
    row = lax.broadcasted_iota(jnp.int32, (SUBLANES, CB), 0)

    h_prev = h_in
    for g in range(n_groups):
        rows = slice(g * SUBLANES, (g + 1) * SUBLANES)
        a = a_ref[rows, :]
        b = b_ref[rows, :]
        for s in (1, 2, 4):
            keep = row >= s
            a_sh = jnp.where(keep, pltpu.roll(a, s, 0), 1.0)
            b_sh = jnp.where(keep, pltpu.roll(b, s, 0), 0.0)
            b = a * b_sh + b
            a = a * a_sh
        h = a * h_prev + b
        h_ref[rows, :] = h
        h_prev = jnp.broadcast_to(h[SUBLANES - 1:SUBLANES, :], (SUBLANES, CB))
    return h_prev


def _norm(x_ref, g_ref, u_ref):
    @pl.when(pl.program_id(1) == 0)
    def _():
        u_ref[...] = _rms(x_ref[...], g_ref[...]).astype(BF16)

    return u_ref[...]


def _gates(u, wga_ref, wgb_ref, sga_ref, sgb_ref):
    sga_ref[...] = jax.nn.sigmoid(
        jnp.dot(u, wga_ref[...], preferred_element_type=F32)).astype(BF16)
    sgb_ref[...] = jax.nn.sigmoid(
        jnp.dot(u, wgb_ref[...], preferred_element_type=F32)).astype(BF16)


def _proj_mix_prompt_kernel(tiles_per_seq, x_ref, g_ref,
                            wabg_ref, wacg_ref, wax_ref, wbx_ref, wbg_ref, wga_ref, wgb_ref,
                            wa_ref, wb_ref, bias_ref, wri_ref, br_ref, bi_ref, lam_ref,
                            ya_ref, yb_ref, sga_ref, sgb_ref, na_ref, nb_ref, hl_ref,
                            u_ref, exta_ref, extb_ref, taila_ref, tailb_ref, hc_ref,
                            a_buf, b_buf, h_buf):
    c = pl.program_id(1)
    u = _norm(x_ref, g_ref, u_ref)
    dot = lambda w_ref: jnp.dot(u, w_ref[...], preferred_element_type=F32)

    @pl.when(pl.program_id(0) % tiles_per_seq == 0)
    def _():
        taila_ref[c] = jnp.zeros((SUBLANES, CA), F32)
        tailb_ref[c] = jnp.zeros((SUBLANES, CB), F32)
        hc_ref[c] = jnp.zeros((SUBLANES, CB), F32)

    pre_ga = dot(wga_ref)
    bx = dot(wbx_ref)
    sga_ref[...] = jax.nn.sigmoid(pre_ga).astype(BF16)

    pre_gb = dot(wgb_ref)
    extb_ref[0:SUBLANES, :] = tailb_ref[c]
    extb_ref[SUBLANES:, :] = bx
    wb = wb_ref[...]
    xb = (extb_ref[SUBLANES - 3:SUBLANES - 3 + TM, :] * wb[0:1]
          + extb_ref[SUBLANES - 2:SUBLANES - 2 + TM, :] * wb[1:2]
          + extb_ref[SUBLANES - 1:SUBLANES - 1 + TM, :] * wb[2:3]
          + bx * wb[3:4]) + bias_ref[...]
    tail_b = extb_ref[TM:TM + SUBLANES, :]
    nb_ref[...] = tail_b
    tailb_ref[c] = tail_b

    heads = range(CB // HEAD)
    pres = [_gate_pre(xb, wri_ref, h) for h in heads]
    acg = dot(wacg_ref)
    sgb_ref[...] = jax.nn.sigmoid(pre_gb).astype(BF16)
    coeffs = functools.partial(_lru_coeffs_head, xb, b_r=br_ref[...], b_i=bi_ref[...],
                               lam=lam_ref[...], a_ref=a_buf, b_ref=b_buf)
    ax = dot(wax_ref)
    coeffs(pre=pres[0], h=0)
    bg = dot(wbg_ref)
    coeffs(pre=pres[1], h=1)
    coeffs(pre=pres[2], h=2)

    z = acg * ax
    exta_ref[0:SUBLANES, :] = taila_ref[c]
    exta_ref[SUBLANES:, :] = z
    wa = wa_ref[...]
    conv = (exta_ref[SUBLANES - 2:SUBLANES - 2 + TM, :] * wa[0:1]
            + exta_ref[SUBLANES - 1:SUBLANES - 1 + TM, :] * wa[1:2]
            + z * wa[2:3])
    tail_a = exta_ref[TM:TM + SUBLANES, :]
    na_ref[...] = tail_a
    taila_ref[c] = tail_a

    coeffs(pre=pres[3], h=3)
    abg = dot(wabg_ref)
    h_out = _scan_groups(a_buf, b_buf, h_buf, hc_ref[c], TM // SUBLANES)
    hc_ref[c] = h_out
    hl_ref[...] = h_out
    yb_ref[...] = (h_buf[...] * jax.nn.gelu(bg)).astype(BF16)
    ya_ref[...] = (abg * conv).astype(BF16)


def _proj_mix_sample_kernel(nb, nsteps, x_ref, g_ref,
                            wabg_ref, wacg_ref, wax_ref, wbx_ref, wbg_ref, wga_ref, wgb_ref,
                            wa_ref, wb_ref, bias_ref, wri_ref, br_ref, bi_ref, lam_ref,
                            sa0_ref, sa1_ref, sb0_ref, sb1_ref, sb2_ref, h0_ref,
                            ya_ref, yb_ref, sga_ref, sgb_ref, na_ref, nb_ref, hl_ref,
                            u_ref, a_buf, b_buf):
    u = _norm(x_ref, g_ref, u_ref)
    _gates(u, wga_ref, wgb_ref, sga_ref, sgb_ref)
    dot = lambda w_ref: jnp.dot(u, w_ref[...], preferred_element_type=F32)
    slab = lambda t: slice(t * nb, (t + 1) * nb)

    z = dot(wacg_ref) * dot(wax_ref)
    abg = dot(wabg_ref)
    ins = [sa0_ref[...], sa1_ref[...]] + [z[slab(t), :] for t in range(nsteps)]
    wa = wa_ref[...]
    for t in range(nsteps):
        conv = ins[t] * wa[0:1] + ins[t + 1] * wa[1:2] + ins[t + 2] * wa[2:3]
        ya_ref[slab(t), :] = (abg[slab(t), :] * conv).astype(BF16)
    for k in range(CONV_A_W - 1):
        na_ref[k] = ins[nsteps + k]

    bx = dot(wbx_ref)
    ins = ([sb0_ref[...], sb1_ref[...], sb2_ref[...]]
           + [bx[slab(t), :] for t in range(nsteps)])
    wb = wb_ref[...]
    xb = jnp.concatenate(
        [(ins[t] * wb[0:1] + ins[t + 1] * wb[1:2] + ins[t + 2] * wb[2:3]
          + ins[t + 3] * wb[3:4]) + bias_ref[...] for t in range(nsteps)], axis=0)
    for k in range(CONV_B_W - 1):
        nb_ref[k] = ins[nsteps + k]

    _lru_coeffs(xb, wri_ref, br_ref[...], bi_ref[...], lam_ref[...], a_buf, b_buf)
    gate = jax.nn.gelu(dot(wbg_ref))
    h = h0_ref[...]
    for t in range(nsteps):
        h = a_buf[slab(t), :] * h + b_buf[slab(t), :]
        yb_ref[slab(t), :] = (h * gate[slab(t), :]).astype(BF16)
    hl_ref[...] = h


def _proj_mix(x, g, w_in16, wa, wb, bias, wri, br, bi, lam, layer, *,
              seq=None, states=None):
    m = x.shape[0]
    w_col = lambda width, off: pl.BlockSpec(
        (None, D_MODEL, width), lambda i, c: (0, 0, off // width + c))
    vec_b = pl.BlockSpec((None, 1, CB), lambda i, c: (layer, 0, c))
    in_specs = [
        pl.BlockSpec((TM, D_MODEL), lambda i, c: (i, 0)),
        pl.BlockSpec((None, 1, D_MODEL), lambda i, c: (layer, 0, 0)),
        w_col(CA, OFF_A_BG), w_col(CA, OFF_A_CG), w_col(CA, OFF_A_X),
        w_col(CB, OFF_B_X), w_col(CB, OFF_B_GATE),
        w_col(CB, OFF_GATE_A), w_col(CB, OFF_GATE_B),
        pl.BlockSpec((None, CONV_A_W, CA), lambda i, c: (layer, 0, c)),
        pl.BlockSpec((None, CONV_B_W, CB), lambda i, c: (layer, 0, c)),
        vec_b,
        pl.BlockSpec((None, CB // HEAD, HEAD, 2 * HEAD), lambda i, c: (layer, c, 0, 0)),
        vec_b, vec_b, vec_b,
    ]
    args = [x, g] + [w_in16] * 7 + [wa, wb, bias, wri, br, bi, lam]
    out_specs = [
        pl.BlockSpec((TM, CA), lambda i, c: (i, c)),
        pl.BlockSpec((TM, CB), lambda i, c: (i, c)),
        pl.BlockSpec((TM, CB), lambda i, c: (i, c)),
        pl.BlockSpec((TM, CB), lambda i, c: (i, c)),
    ]
    out_shape = [
        jax.ShapeDtypeStruct((m, D_A), BF16),
        jax.ShapeDtypeStruct((m, D_B), BF16),
        jax.ShapeDtypeStruct((m, D_MODEL), BF16),
        jax.ShapeDtypeStruct((m, D_MODEL), BF16),
    ]
    scratch = [pltpu.VMEM((TM, D_MODEL), BF16)]
    if states is None:
        tiles_per_seq = seq // TM
        body = functools.partial(_proj_mix_prompt_kernel, tiles_per_seq)
        out_specs += [
            pl.BlockSpec((None, SUBLANES, CA), lambda i, c: (i, 0, c)),
            pl.BlockSpec((None, SUBLANES, CB), lambda i, c: (i, 0, c)),
            pl.BlockSpec((None, SUBLANES, CB), lambda i, c: (i, 0, c)),
        ]
        out_shape += [
            jax.ShapeDtypeStruct((m // TM, SUBLANES, D_A), F32),
            jax.ShapeDtypeStruct((m // TM, SUBLANES, D_B), F32),
            jax.ShapeDtypeStruct((m // TM, SUBLANES, D_B), F32),
        ]
        scratch += [
            pltpu.VMEM((SUBLANES + TM, CA), F32),
            pltpu.VMEM((SUBLANES + TM, CB), F32),
            pltpu.VMEM((MIX_BLOCKS, SUBLANES, CA), F32),
            pltpu.VMEM((MIX_BLOCKS, SUBLANES, CB), F32),
            pltpu.VMEM((MIX_BLOCKS, SUBLANES, CB), F32),
            pltpu.VMEM((TM, CB), F32),
            pltpu.VMEM((TM, CB), F32),
            pltpu.VMEM((TM, CB), F32),
        ]
        name = "proj_mix_prompt"
    else:
        sa, sb, h0, nb, nsteps = states
        assert m == TM == nb * nsteps
        body = functools.partial(_proj_mix_sample_kernel, nb, nsteps)
        st_a = lambda k: pl.BlockSpec(
            (None, nb, CA), lambda i, c: (layer, 0, k * MIX_BLOCKS + c))
        st_b = lambda k: pl.BlockSpec(
            (None, nb, CB), lambda i, c: (layer, 0, k * MIX_BLOCKS + c))
        in_specs += [st_a(0), st_a(1), st_b(0), st_b(1), st_b(2),
                     pl.BlockSpec((None, nb, CB), lambda i, c: (layer, 0, c))]
        args += [sa, sa, sb, sb, sb, h0]
        out_specs += [
            pl.BlockSpec((CONV_A_W - 1, nb, CA), lambda i, c: (0, 0, c)),
            pl.BlockSpec((CONV_B_W - 1, nb, CB), lambda i, c: (0, 0, c)),
            pl.BlockSpec((nb, CB), lambda i, c: (0, c)),
        ]
        out_shape += [
            jax.ShapeDtypeStruct((CONV_A_W - 1, nb, D_A), F32),
            jax.ShapeDtypeStruct((CONV_B_W - 1, nb, D_B), F32),
            jax.ShapeDtypeStruct((nb, D_B), F32),
        ]
        scratch += [pltpu.VMEM((TM, CB), F32), pltpu.VMEM((TM, CB), F32)]
        name = "proj_mix_sample"
    return pl.pallas_call(
        body,
        grid=(m // TM, MIX_BLOCKS),
        in_specs=in_specs,
        out_specs=out_specs,
        out_shape=out_shape,
        scratch_shapes=scratch,
        compiler_params=_params(2),
        name=name,
    )(*args)


def _merge_kernel(ya_ref, yb_ref, sga_ref, sgb_ref, woa_ref, wob_ref, wo_ref,
                  x_ref, g_ref, o_ref, acc_ref):
    j = pl.program_id(1)

    @pl.when(j == 0)
    def _():
        acc_ref[...] = jnp.zeros_like(acc_ref)

    y_a = jnp.dot(ya_ref[...], woa_ref[...], preferred_element_type=F32)
    y_b = jnp.dot(yb_ref[...], wob_ref[...], preferred_element_type=F32)
    merged = sga_ref[...].astype(F32) * y_a + sgb_ref[...].astype(F32) * y_b
    acc_ref[...] += jnp.dot(merged.astype(BF16), wo_ref[...], preferred_element_type=F32)

    @pl.when(j == pl.num_programs(1) - 1)
    def _():
        o_ref[...] = x_ref[...] + _rms(acc_ref[...], g_ref[...])


def _merge(ya, yb, sga, sgb, woa, wob, wo, x, g, layer):
    m = x.shape[0]
    tc = TC_MERGE
    return pl.pallas_call(
        _merge_kernel,
        grid=(m // TM, D_MODEL // tc),
        in_specs=[
            pl.BlockSpec((TM, D_A), lambda i, j: (i, 0)),
            pl.BlockSpec((TM, D_B), lambda i, j: (i, 0)),
            pl.BlockSpec((TM, tc), lambda i, j: (i, j)),
            pl.BlockSpec((TM, tc), lambda i, j: (i, j)),
            pl.BlockSpec((None, D_A, tc), lambda i, j: (0, 0, j)),
            pl.BlockSpec((None, D_B, tc), lambda i, j: (0, 0, j)),
            pl.BlockSpec((None, tc, D_MODEL), lambda i, j: (0, j, 0)),
            pl.BlockSpec((TM, D_MODEL), lambda i, j: (i, 0)),
            pl.BlockSpec((None, 1, D_MODEL), lambda i, j: (layer, 0, 0)),
        ],
        out_specs=pl.BlockSpec((TM, D_MODEL), lambda i, j: (i, 0)),
        out_shape=jax.ShapeDtypeStruct((m, D_MODEL), F32),
        scratch_shapes=[pltpu.VMEM((TM, D_MODEL), F32)],
        compiler_params=_params(2),
        name="merge",
    )(ya, yb, sga, sgb, woa, wob, wo, x, g)


def _mlp_kernel(n_cast, x_ref, gpre_ref, wg_ref, wu_ref, wd_ref, gpost_ref, *refs):
    cast_in, o_ref = refs[:n_cast], refs[n_cast]
    cast_out = refs[n_cast + 1:2 * n_cast + 1]
    v_ref, acc_ref = refs[2 * n_cast + 1:]
    j = pl.program_id(1)

    @pl.when(j == 0)
    def _():
        v_ref[...] = _rms(x_ref[...], gpre_ref[...]).astype(BF16)
        acc_ref[...] = jnp.zeros_like(acc_ref)

    v = v_ref[...]
    g = jnp.dot(v, wg_ref[...], preferred_element_type=F32)
    for src, dst in zip(cast_in, cast_out):
        dst[...] = src[...].astype(BF16)
    u = jnp.dot(v, wu_ref[...], preferred_element_type=F32)
    hidden = (g * jax.nn.sigmoid(g)) * u
    acc_ref[...] += jnp.dot(hidden.astype(BF16), wd_ref[...], preferred_element_type=F32)

    @pl.when(j == pl.num_programs(1) - 1)
    def _():
        o_ref[...] = x_ref[...] + _rms(acc_ref[...], gpost_ref[...])


def _mlp(x, gpre, wgu, wd, gpost, layer, cast_next=None):
    m = x.shape[0]
    nh = FFN_HIDDEN // TH_MLP
    n_i = m // TM
    in_specs = [
        pl.BlockSpec((TM, D_MODEL), lambda i, j: (i, 0)),
        pl.BlockSpec((None, 1, D_MODEL), lambda i, j: (layer, 0, 0)),
        pl.BlockSpec((None, D_MODEL, TH_MLP), lambda i, j: (0, 0, j)),
        pl.BlockSpec((None, D_MODEL, TH_MLP), lambda i, j: (0, 0, nh + j)),
        pl.BlockSpec((None, TH_MLP, D_MODEL), lambda i, j: (0, j, 0)),
        pl.BlockSpec((None, 1, D_MODEL), lambda i, j: (layer, 0, 0)),
    ]
    out_specs = [pl.BlockSpec((TM, D_MODEL), lambda i, j: (i, 0))]
    out_shape = [jax.ShapeDtypeStruct((m, D_MODEL), F32)]
    args = [x, gpre, wgu, wgu, wd, gpost]
    n_cast = 0
    if cast_next is not None:
        nxt = layer + 1
        w_in, w_gu, w_down, w_oa, w_ob, w_o = cast_next
        sq = D_MODEL // SQUARE_CAST_COLS
        assert sq <= nh
        chunks = [
            (w_in, n_i, nh, lambda i, j: (i, j)),
            (w_gu, n_i, nh, lambda i, j: (i, j)),
            (w_down, nh, n_i, lambda i, j: (j, i)),
            (w_oa, n_i, sq, lambda i, j: (i, jnp.minimum(j, sq - 1))),
            (w_ob, n_i, sq, lambda i, j: (i, jnp.minimum(j, sq - 1))),
            (w_o, n_i, sq, lambda i, j: (i, jnp.minimum(j, sq - 1))),
        ]
        n_cast = len(chunks)
        for w, row_blocks, col_blocks, idx in chunks:
            rows, cols = w.shape[1] // row_blocks, w.shape[2] // col_blocks
            assert (rows * row_blocks, cols * col_blocks) == w.shape[1:], w.shape
            assert rows % BF16_ROWS == 0 and cols % LANES == 0, (w.shape, rows, cols)
            in_specs.append(pl.BlockSpec(
                (None, rows, cols), lambda i, j, idx=idx: (nxt,) + idx(i, j)))
            out_specs.append(pl.BlockSpec(
                (None, rows, cols), lambda i, j, idx=idx: (0,) + idx(i, j)))
            out_shape.append(jax.ShapeDtypeStruct((1,) + w.shape[1:], BF16))
            args.append(w)
    outs = pl.pallas_call(
        functools.partial(_mlp_kernel, n_cast),
        grid=(n_i, nh),
        in_specs=in_specs,
        out_specs=out_specs,
        out_shape=out_shape,
        scratch_shapes=[pltpu.VMEM((TM, D_MODEL), BF16), pltpu.VMEM((TM, D_MODEL), F32)],
        compiler_params=_params(2),
        name="mlp_cast" if n_cast else "mlp",
    )(*args)
    return outs[0], tuple(outs[1:])


def kernel(x_prompt, x_sample, state_conv_a, state_conv_b, state_lru_h, w_in, conv_a_w,
           w_out_a, conv_b_w, conv_b_bias, w_r, b_r, w_i, b_i, lru_lambda, w_out_b, w_o,
           norm_pre_mix, norm_post_mix, norm_pre_ffn, norm_post_ffn, w_gate_up, w_down):
    batch, seq, _ = x_prompt.shape
    nb, nsteps, _ = x_sample.shape
    depth = w_in.shape[0]
    assert seq % TM == 0 and nb * nsteps == TM
    assert nsteps >= CONV_B_W - 1

    big_w = (w_in, w_gate_up, w_down, w_out_a, w_out_b, w_o)
    w16 = tuple(w[:1].astype(BF16) for w in big_w)
    wri16 = jnp.concatenate([w_r, w_i], axis=-1).astype(BF16)
    row = lambda v: v.reshape(depth, 1, v.shape[-1])
    bias, br, bi, lam = row(conv_b_bias), row(b_r), row(b_i), row(lru_lambda)
    g_pre_mix, g_post_mix = row(norm_pre_mix), row(norm_post_mix)
    g_pre_ffn, g_post_ffn = row(norm_pre_ffn), row(norm_post_ffn)
    sa = state_conv_a.reshape(depth, nb, (CONV_A_W - 1) * D_A)
    sb = state_conv_b.reshape(depth, nb, (CONV_B_W - 1) * D_B)

    xp = x_prompt.reshape(batch * seq, D_MODEL)
    xs = x_sample.transpose(1, 0, 2).reshape(nsteps * nb, D_MODEL)

    pa, pb, ph, s_a, s_b, s_h = [], [], [], [], [], []
    for l in range(depth):
        w_in16, w_gu16, w_down16, w_out_a16, w_out_b16, w_o16 = w16
        mix_w = (w_in16, conv_a_w, conv_b_w, bias, wri16, br, bi, lam, l)
        ya_p, yb_p, sga_p, sgb_p, na_p, nb_p, hl_p = _proj_mix(
            xp, g_pre_mix, *mix_w, seq=seq)
        ya_s, yb_s, sga_s, sgb_s, na_s, nb_s, hl_s = _proj_mix(
            xs, g_pre_mix, *mix_w, states=(sa, sb, state_lru_h, nb, nsteps))
        out_w = (w_out_a16, w_out_b16, w_o16)
        xp = _merge(ya_p, yb_p, sga_p, sgb_p, *out_w, xp, g_post_mix, l)
        xs = _merge(ya_s, yb_s, sga_s, sgb_s, *out_w, xs, g_post_mix, l)
        xp, w16 = _mlp(xp, g_pre_ffn, w_gu16, w_down16, g_post_ffn, l,
                       cast_next=big_w if l + 1 < depth else None)
        xs, _ = _mlp(xs, g_pre_ffn, w_gu16, w_down16, g_post_ffn, l)
        last = slice(seq // TM - 1, None, seq // TM)
        pa.append(na_p[last, SUBLANES - (CONV_A_W - 1):, :])
        pb.append(nb_p[last, SUBLANES - (CONV_B_W - 1):, :])
        ph.append(hl_p[last, 0, :])
        s_a.append(na_s)
        s_b.append(nb_s)
        s_h.append(hl_s)

    y_prompt = xp.reshape(batch, seq, D_MODEL)
    y_sample = xs.reshape(nsteps, nb, D_MODEL).transpose(1, 0, 2)
    return (y_prompt, y_sample, jnp.stack(pa), jnp.stack(pb), jnp.stack(ph),
            jnp.stack(s_a).transpose(0, 2, 1, 3), jnp.stack(s_b).transpose(0, 2, 1, 3),
            jnp.stack(s_h))
```

```python
import functools

import jax
import jax.numpy as jnp
from jax import lax
from jax.experimental import pallas as pl
from jax.experimental.pallas import tpu as pltpu

F32 = jnp.float32
BF16 = jnp.bfloat16

D_MODEL = 2048
D_A = D_MODEL // 2
D_B = D_MODEL
CONV_A_W = 3
CONV_B_W = 4
HEAD = 128
C_LRU = 8.0
LOG2_E = 1.4426950408889634
FFN_HIDDEN = 5632
EPS = 1e-6
IN_COLS = 3 * D_A + 2 * D_B + 2 * D_MODEL

OFF_A_BG, OFF_A_CG, OFF_A_X = 0, D_A, 2 * D_A
OFF_B_X = 3 * D_A
OFF_B_GATE = OFF_B_X + D_B
OFF_GATE_A = OFF_B_GATE + D_B
OFF_GATE_B = OFF_GATE_A + D_MODEL

SUBLANES = 8
BF16_ROWS = 16
LANES = 128
TM = 512
TC_MERGE = 512
TM_MERGE = 256
TH_MLP = 512
SQUARE_CAST_COLS = 256
MIX_BLOCKS = 4
CA = D_A // MIX_BLOCKS
CB = D_B // MIX_BLOCKS
VMEM_LIMIT = 56 * 1024 * 1024


def _params(n_axes):
    return pltpu.CompilerParams(
        dimension_semantics=("arbitrary",) * n_axes,
        vmem_limit_bytes=VMEM_LIMIT)


def _rms(x, g):
    var = jnp.mean(x * x, axis=-1, keepdims=True)
    return x * lax.rsqrt(var + EPS) * g


def _sigmoid(x):
    return 0.5 * jnp.tanh(0.5 * x) + 0.5


def _log_sigmoid(x):
    return jnp.minimum(x, 0.0) - jnp.log1p(jnp.exp(-jnp.abs(x)))


def _head(h):
    return slice(h * HEAD, (h + 1) * HEAD)


def _gate_pre(xb, wri_ref, h):
    return jnp.dot(xb[:, _head(h)].astype(BF16), wri_ref[h], preferred_element_type=F32)


def _lru_coeffs_head(xb, pre, b_r, b_i, lam, a_ref, b_ref, h):
    sl = _head(h)
    r = _sigmoid(pre[:, :HEAD] + b_r[:, sl])
    i = _sigmoid(pre[:, HEAD:] + b_i[:, sl])
    a = jnp.exp2(r * ((C_LRU * LOG2_E) * _log_sigmoid(lam[:, sl])))
    y = 1.0 - a * a
    mult = jnp.where(y > 0.0, y * lax.rsqrt(y), 0.0)
    a_ref[:, sl] = a
    b_ref[:, sl] = mult * i * xb[:, sl]


def _lru_coeffs(xb, wri_ref, b_r, b_i, lam, a_ref, b_ref):
    for h in range(CB // HEAD):
        _lru_coeffs_head(xb, _gate_pre(xb, wri_ref, h), b_r, b_i, lam, a_ref, b_ref, h)


def _scan_groups(a_ref, b_ref, h_ref, h_in, n_groups):
    row = lax.broadcasted_iota(jnp.int32, (SUBLANES, CB), 0)

    h_prev = h_in
    for g in range(n_groups):
        rows = slice(g * SUBLANES, (g + 1) * SUBLANES)
        a = a_ref[rows, :]
        b = b_ref[rows, :]
        for s in (1, 2, 4):
            keep = row >= s
            a_sh = jnp.where(keep, pltpu.roll(a, s, 0), 1.0)
            b_sh = jnp.where(keep, pltpu.roll(b, s, 0), 0.0)
            b = a * b_sh + b
            a = a * a_sh
        h = a * h_prev + b
        h_ref[rows, :] = h
        h_prev = jnp.broadcast_to(h[SUBLANES - 1:SUBLANES, :], (SUBLANES, CB))
    return h_prev


def _norm(x_ref, g_ref, u_ref):
    @pl.when(pl.program_id(1) == 0)
    def _():
        u_ref[...] = _rms(x_ref[...], g_ref[...]).astype(BF16)

    return u_ref[...]


def _gates(u, wga_ref, wgb_ref, sga_ref, sgb_ref):
    sga_ref[...] = _sigmoid(
        jnp.dot(u, wga_ref[...], preferred_element_type=F32)).astype(BF16)
    sgb_ref[...] = _sigmoid(
        jnp.dot(u, wgb_ref[...], preferred_element_type=F32)).astype(BF16)


def _cast_operands(chunks, src_layer):
    in_specs, out_specs, out_shape, args = [], [], [], []
    for w, row_blocks, col_blocks, idx in chunks:
        rows, cols = w.shape[1] // row_blocks, w.shape[2] // col_blocks
        assert (rows * row_blocks, cols * col_blocks) == w.shape[1:], w.shape
        assert rows % BF16_ROWS == 0 and cols % LANES == 0, (w.shape, rows, cols)
        in_specs.append(pl.BlockSpec(
            (None, rows, cols), lambda i, j, idx=idx: (src_layer,) + idx(i, j)))
        out_specs.append(pl.BlockSpec(
            (None, rows, cols), lambda i, j, idx=idx: (0,) + idx(i, j)))
        out_shape.append(jax.ShapeDtypeStruct((1,) + w.shape[1:], BF16))
        args.append(w)
    return in_specs, out_specs, out_shape, args


def _cast_blocks(cast_in, cast_out):
    for src, dst in zip(cast_in, cast_out):
        dst[...] = src[...].astype(BF16)


PROJ_MIX_INPUTS = 16
PROJ_MIX_OUTPUTS = 7


def _proj_mix_prompt_kernel(tiles_per_seq, n_cast, *refs):
    (x_ref, g_ref, wabg_ref, wacg_ref, wax_ref, wbx_ref, wbg_ref, wga_ref, wgb_ref,
     wa_ref, wb_ref, bias_ref, wri_ref, br_ref, bi_ref, lam_ref) = refs[:PROJ_MIX_INPUTS]
    refs = refs[PROJ_MIX_INPUTS:]
    cast_in, refs = refs[:n_cast], refs[n_cast:]
    ya_ref, yb_ref, sga_ref, sgb_ref, na_ref, nb_ref, hl_ref = refs[:PROJ_MIX_OUTPUTS]
    refs = refs[PROJ_MIX_OUTPUTS:]
    cast_out, refs = refs[:n_cast], refs[n_cast:]
    (u_ref, exta_ref, extb_ref, taila_ref, tailb_ref, hc_ref, a_buf, b_buf, h_buf) = refs
    c = pl.program_id(1)
    u = _norm(x_ref, g_ref, u_ref)
    dot = lambda w_ref: jnp.dot(u, w_ref[...], preferred_element_type=F32)

    @pl.when(pl.program_id(0) % tiles_per_seq == 0)
    def _():
        taila_ref[c] = jnp.zeros((SUBLANES, CA), F32)
        tailb_ref[c] = jnp.zeros((SUBLANES, CB), F32)
        hc_ref[c] = jnp.zeros((SUBLANES, CB), F32)

    pre_ga = dot(wga_ref)
    _cast_blocks(cast_in, cast_out)
    bx = dot(wbx_ref)
    sga_ref[...] = _sigmoid(pre_ga).astype(BF16)

    pre_gb = dot(wgb_ref)
    extb_ref[0:SUBLANES, :] = tailb_ref[c]
    extb_ref[SUBLANES:, :] = bx
    wb = wb_ref[...]
    xb = (extb_ref[SUBLANES - 3:SUBLANES - 3 + TM, :] * wb[0:1]
          + extb_ref[SUBLANES - 2:SUBLANES - 2 + TM, :] * wb[1:2]
          + extb_ref[SUBLANES - 1:SUBLANES - 1 + TM, :] * wb[2:3]
          + bx * wb[3:4]) + bias_ref[...]
    tail_b = extb_ref[TM:TM + SUBLANES, :]
    nb_ref[...] = tail_b
    tailb_ref[c] = tail_b

    heads = range(CB // HEAD)
    pres = [_gate_pre(xb, wri_ref, h) for h in heads]
    acg = dot(wacg_ref)
    sgb_ref[...] = _sigmoid(pre_gb).astype(BF16)
    coeffs = functools.partial(_lru_coeffs_head, xb, b_r=br_ref[...], b_i=bi_ref[...],
                               lam=lam_ref[...], a_ref=a_buf, b_ref=b_buf)
    ax = dot(wax_ref)
    coeffs(pre=pres[0], h=0)
    bg = dot(wbg_ref)
    coeffs(pre=pres[1], h=1)
    coeffs(pre=pres[2], h=2)

    z = acg * ax
    exta_ref[0:SUBLANES, :] = taila_ref[c]
    exta_ref[SUBLANES:, :] = z
    wa = wa_ref[...]
    conv = (exta_ref[SUBLANES - 2:SUBLANES - 2 + TM, :] * wa[0:1]
            + exta_ref[SUBLANES - 1:SUBLANES - 1 + TM, :] * wa[1:2]
            + z * wa[2:3])
    tail_a = exta_ref[TM:TM + SUBLANES, :]
    na_ref[...] = tail_a
    taila_ref[c] = tail_a

    coeffs(pre=pres[3], h=3)
    abg = dot(wabg_ref)
    h_out = _scan_groups(a_buf, b_buf, h_buf, hc_ref[c], TM // SUBLANES)
    hc_ref[c] = h_out
    hl_ref[...] = h_out
    yb_ref[...] = (h_buf[...] * jax.nn.gelu(bg)).astype(BF16)
    ya_ref[...] = (abg * conv).astype(BF16)


def _proj_mix_sample_kernel(nb, nsteps, x_ref, g_ref,
                            wabg_ref, wacg_ref, wax_ref, wbx_ref, wbg_ref, wga_ref, wgb_ref,
                            wa_ref, wb_ref, bias_ref, wri_ref, br_ref, bi_ref, lam_ref,
                            sa0_ref, sa1_ref, sb0_ref, sb1_ref, sb2_ref, h0_ref,
                            ya_ref, yb_ref, sga_ref, sgb_ref, na_ref, nb_ref, hl_ref,
                            u_ref, a_buf, b_buf):
    u = _norm(x_ref, g_ref, u_ref)
    _gates(u, wga_ref, wgb_ref, sga_ref, sgb_ref)
    dot = lambda w_ref: jnp.dot(u, w_ref[...], preferred_element_type=F32)
    slab = lambda t: slice(t * nb, (t + 1) * nb)

    z = dot(wacg_ref) * dot(wax_ref)
    abg = dot(wabg_ref)
    ins = [sa0_ref[...], sa1_ref[...]] + [z[slab(t), :] for t in range(nsteps)]
    wa = wa_ref[...]
    for t in range(nsteps):
        conv = ins[t] * wa[0:1] + ins[t + 1] * wa[1:2] + ins[t + 2] * wa[2:3]
        ya_ref[slab(t), :] = (abg[slab(t), :] * conv).astype(BF16)
    for k in range(CONV_A_W - 1):
        na_ref[k] = ins[nsteps + k]

    bx = dot(wbx_ref)
    ins = ([sb0_ref[...], sb1_ref[...], sb2_ref[...]]
           + [bx[slab(t), :] for t in range(nsteps)])
    wb = wb_ref[...]
    xb = jnp.concatenate(
        [(ins[t] * wb[0:1] + ins[t + 1] * wb[1:2] + ins[t + 2] * wb[2:3]
          + ins[t + 3] * wb[3:4]) + bias_ref[...] for t in range(nsteps)], axis=0)
    for k in range(CONV_B_W - 1):
        nb_ref[k] = ins[nsteps + k]

    _lru_coeffs(xb, wri_ref, br_ref[...], bi_ref[...], lam_ref[...], a_buf, b_buf)
    gate = jax.nn.gelu(dot(wbg_ref))
    h = h0_ref[...]
    for t in range(nsteps):
        h = a_buf[slab(t), :] * h + b_buf[slab(t), :]
        yb_ref[slab(t), :] = (h * gate[slab(t), :]).astype(BF16)
    hl_ref[...] = h


def _proj_mix(x, g, w_in16, wa, wb, bias, wri, br, bi, lam, layer, *,
              seq=None, states=None, cast_same=None):
    m = x.shape[0]
    n_i = m // TM
    w_col = lambda width, off: pl.BlockSpec(
        (None, D_MODEL, width), lambda i, c: (0, 0, off // width + c))
    vec_b = pl.BlockSpec((None, 1, CB), lambda i, c: (layer, 0, c))
    in_specs = [
        pl.BlockSpec((TM, D_MODEL), lambda i, c: (i, 0)),
        pl.BlockSpec((None, 1, D_MODEL), lambda i, c: (layer, 0, 0)),
        w_col(CA, OFF_A_BG), w_col(CA, OFF_A_CG), w_col(CA, OFF_A_X),
        w_col(CB, OFF_B_X), w_col(CB, OFF_B_GATE),
        w_col(CB, OFF_GATE_A), w_col(CB, OFF_GATE_B),
        pl.BlockSpec((None, CONV_A_W, CA), lambda i, c: (layer, 0, c)),
        pl.BlockSpec((None, CONV_B_W, CB), lambda i, c: (layer, 0, c)),
        vec_b,
        pl.BlockSpec((None, CB // HEAD, HEAD, 2 * HEAD), lambda i, c: (layer, c, 0, 0)),
        vec_b, vec_b, vec_b,
    ]
    args = [x, g] + [w_in16] * 7 + [wa, wb, bias, wri, br, bi, lam]
    out_specs = [
        pl.BlockSpec((TM, CA), lambda i, c: (i, c)),
        pl.BlockSpec((TM, CB), lambda i, c: (i, c)),
        pl.BlockSpec((TM, CB), lambda i, c: (i, c)),
        pl.BlockSpec((TM, CB), lambda i, c: (i, c)),
    ]
    out_shape = [
        jax.ShapeDtypeStruct((m, D_A), BF16),
        jax.ShapeDtypeStruct((m, D_B), BF16),
        jax.ShapeDtypeStruct((m, D_MODEL), BF16),
        jax.ShapeDtypeStruct((m, D_MODEL), BF16),
    ]
    scratch = [pltpu.VMEM((TM, D_MODEL), BF16)]
    if states is None:
        tiles_per_seq = seq // TM
        n_cast = 0
        if cast_same is not None:
            chunks = [(w, n_i, MIX_BLOCKS, lambda i, c: (i, c)) for w in cast_same]
            n_cast = len(chunks)
            c_in, c_out, c_shape, c_args = _cast_operands(chunks, layer)
        body = functools.partial(_proj_mix_prompt_kernel, tiles_per_seq, n_cast)
        out_specs += [
            pl.BlockSpec((None, SUBLANES, CA), lambda i, c: (i, 0, c)),
            pl.BlockSpec((None, SUBLANES, CB), lambda i, c: (i, 0, c)),
            pl.BlockSpec((None, SUBLANES, CB), lambda i, c: (i, 0, c)),
        ]
        out_shape += [
            jax.ShapeDtypeStruct((m // TM, SUBLANES, D_A), F32),
            jax.ShapeDtypeStruct((m // TM, SUBLANES, D_B), F32),
            jax.ShapeDtypeStruct((m // TM, SUBLANES, D_B), F32),
        ]
        scratch += [
            pltpu.VMEM((SUBLANES + TM, CA), F32),
            pltpu.VMEM((SUBLANES + TM, CB), F32),
            pltpu.VMEM((MIX_BLOCKS, SUBLANES, CA), F32),
            pltpu.VMEM((MIX_BLOCKS, SUBLANES, CB), F32),
            pltpu.VMEM((MIX_BLOCKS, SUBLANES, CB), F32),
            pltpu.VMEM((TM, CB), F32),
            pltpu.VMEM((TM, CB), F32),
            pltpu.VMEM((TM, CB), F32),
        ]
        assert (len(in_specs), len(out_specs)) == (PROJ_MIX_INPUTS, PROJ_MIX_OUTPUTS)
        if n_cast:
            in_specs, args = in_specs + c_in, args + c_args
            out_specs, out_shape = out_specs + c_out, out_shape + c_shape
        name = "proj_mix_prompt_cast" if n_cast else "proj_mix_prompt"
    else:
        sa, sb, h0, nb, nsteps = states
        assert m == TM == nb * nsteps
        body = functools.partial(_proj_mix_sample_kernel, nb, nsteps)
        st_a = lambda k: pl.BlockSpec(
            (None, nb, CA), lambda i, c: (layer, 0, k * MIX_BLOCKS + c))
        st_b = lambda k: pl.BlockSpec(
            (None, nb, CB), lambda i, c: (layer, 0, k * MIX_BLOCKS + c))
        in_specs += [st_a(0), st_a(1), st_b(0), st_b(1), st_b(2),
                     pl.BlockSpec((None, nb, CB), lambda i, c: (layer, 0, c))]
        args += [sa, sa, sb, sb, sb, h0]
        out_specs += [
            pl.BlockSpec((CONV_A_W - 1, nb, CA), lambda i, c: (0, 0, c)),
            pl.BlockSpec((CONV_B_W - 1, nb, CB), lambda i, c: (0, 0, c)),
            pl.BlockSpec((nb, CB), lambda i, c: (0, c)),
        ]
        out_shape += [
            jax.ShapeDtypeStruct((CONV_A_W - 1, nb, D_A), F32),
            jax.ShapeDtypeStruct((CONV_B_W - 1, nb, D_B), F32),
            jax.ShapeDtypeStruct((nb, D_B), F32),
        ]
        scratch += [pltpu.VMEM((TM, CB), F32), pltpu.VMEM((TM, CB), F32)]
        name = "proj_mix_sample"
    return pl.pallas_call(
        body,
        grid=(m // TM, MIX_BLOCKS),
        in_specs=in_specs,
        out_specs=out_specs,
        out_shape=out_shape,
        scratch_shapes=scratch,
        compiler_params=_params(2),
        name=name,
    )(*args)


def _merge_kernel(ya_ref, yb_ref, sga_ref, sgb_ref, woa_ref, wob_ref, wo_ref,
                  x_ref, g_ref, o_ref):
    tc = TC_MERGE
    ya, yb = ya_ref[...], yb_ref[...]
    mix = None
    for jb in range(D_MODEL // tc):
        cols = slice(jb * tc, (jb + 1) * tc)
        y_a = jnp.dot(ya, woa_ref[:, cols], preferred_element_type=F32)
        y_b = jnp.dot(yb, wob_ref[:, cols], preferred_element_type=F32)
        merged = (sga_ref[:, cols].astype(F32) * y_a
                  + sgb_ref[:, cols].astype(F32) * y_b).astype(BF16)
        part = jnp.dot(merged, wo_ref[cols, :], preferred_element_type=F32)
        mix = part if mix is None else mix + part
    o_ref[...] = x_ref[...] + _rms(mix, g_ref[...])


def _merge(ya, yb, sga, sgb, woa, wob, wo, x, g, layer):
    m = x.shape[0]
    tm = TM_MERGE
    rows = lambda width: pl.BlockSpec((tm, width), lambda i: (i, 0))
    resident = lambda k, n: pl.BlockSpec(
        (None, k, n), lambda i: (0, 0, 0), pipeline_mode=pl.Buffered(1))
    return pl.pallas_call(
        _merge_kernel,
        grid=(m // tm,),
        in_specs=[
            rows(D_A), rows(D_B), rows(D_MODEL), rows(D_MODEL),
            resident(D_A, D_MODEL), resident(D_B, D_MODEL), resident(D_MODEL, D_MODEL),
            rows(D_MODEL),
            pl.BlockSpec((None, 1, D_MODEL), lambda i: (layer, 0, 0)),
        ],
        out_specs=rows(D_MODEL),
        out_shape=jax.ShapeDtypeStruct((m, D_MODEL), F32),
        compiler_params=_params(1),
        name="merge",
    )(ya, yb, sga, sgb, woa, wob, wo, x, g)


def _mlp_kernel(n_cast, x_ref, gpre_ref, wg_ref, wu_ref, wd_ref, gpost_ref, *refs):
    cast_in, o_ref = refs[:n_cast], refs[n_cast]
    cast_out = refs[n_cast + 1:2 * n_cast + 1]
    v_ref, acc_ref = refs[2 * n_cast + 1:]
    j = pl.program_id(1)

    @pl.when(j == 0)
    def _():
        v_ref[...] = _rms(x_ref[...], gpre_ref[...]).astype(BF16)
        acc_ref[...] = jnp.zeros_like(acc_ref)

    v = v_ref[...]
    g = jnp.dot(v, wg_ref[...], preferred_element_type=F32)
    _cast_blocks(cast_in, cast_out)
    u = jnp.dot(v, wu_ref[...], preferred_element_type=F32)
    hidden = (g * _sigmoid(g)) * u
    acc_ref[...] += jnp.dot(hidden.astype(BF16), wd_ref[...], preferred_element_type=F32)

    @pl.when(j == pl.num_programs(1) - 1)
    def _():
        o_ref[...] = x_ref[...] + _rms(acc_ref[...], gpost_ref[...])


def _mlp(x, gpre, wgu, wd, gpost, layer, cast_next=None):
    m = x.shape[0]
    nh = FFN_HIDDEN // TH_MLP
    n_i = m // TM
    in_specs = [
        pl.BlockSpec((TM, D_MODEL), lambda i, j: (i, 0)),
        pl.BlockSpec((None, 1, D_MODEL), lambda i, j: (layer, 0, 0)),
        pl.BlockSpec((None, D_MODEL, TH_MLP), lambda i, j: (0, 0, j)),
        pl.BlockSpec((None, D_MODEL, TH_MLP), lambda i, j: (0, 0, nh + j)),
        pl.BlockSpec((None, TH_MLP, D_MODEL), lambda i, j: (0, j, 0)),
        pl.BlockSpec((None, 1, D_MODEL), lambda i, j: (layer, 0, 0)),
    ]
    out_specs = [pl.BlockSpec((TM, D_MODEL), lambda i, j: (i, 0))]
    out_shape = [jax.ShapeDtypeStruct((m, D_MODEL), F32)]
    args = [x, gpre, wgu, wgu, wd, gpost]
    n_cast = 0
    if cast_next is not None:
        nxt = layer + 1
        w_in, w_gu, w_down, w_oa, w_ob, w_o = cast_next
        sq = D_MODEL // SQUARE_CAST_COLS
        assert sq <= nh
        chunks = [
            (w_in, n_i, nh, lambda i, j: (i, j)),
            (w_gu, n_i, nh, lambda i, j: (i, j)),
            (w_down, nh, n_i, lambda i, j: (j, i)),
            (w_oa, n_i, sq, lambda i, j: (i, jnp.minimum(j, sq - 1))),
            (w_ob, n_i, sq, lambda i, j: (i, jnp.minimum(j, sq - 1))),
            (w_o, n_i, sq, lambda i, j: (i, jnp.minimum(j, sq - 1))),
        ]
        n_cast = len(chunks)
        c_in, c_out, c_shape, c_args = _cast_operands(chunks, nxt)
        in_specs, args = in_specs + c_in, args + c_args
        out_specs, out_shape = out_specs + c_out, out_shape + c_shape
    outs = pl.pallas_call(
        functools.partial(_mlp_kernel, n_cast),
        grid=(n_i, nh),
        in_specs=in_specs,
        out_specs=out_specs,
        out_shape=out_shape,
        scratch_shapes=[pltpu.VMEM((TM, D_MODEL), BF16), pltpu.VMEM((TM, D_MODEL), F32)],
        compiler_params=_params(2),
        name="mlp_cast" if n_cast else "mlp",
    )(*args)
    return outs[0], tuple(outs[1:])


def kernel(x_prompt, x_sample, state_conv_a, state_conv_b, state_lru_h, w_in, conv_a_w,
           w_out_a, conv_b_w, conv_b_bias, w_r, b_r, w_i, b_i, lru_lambda, w_out_b, w_o,
           norm_pre_mix, norm_post_mix, norm_pre_ffn, norm_post_ffn, w_gate_up, w_down):
    batch, seq, _ = x_prompt.shape
    nb, nsteps, _ = x_sample.shape
    depth = w_in.shape[0]
    assert seq % TM == 0 and nb * nsteps == TM
    assert nsteps >= CONV_B_W - 1

    big_w = (w_in, w_gate_up, w_down, w_out_a, w_out_b, w_o)
    w16 = (w_in[:1].astype(BF16),)
    wri16 = jnp.concatenate([w_r, w_i], axis=-1).astype(BF16)
    row = lambda v: v.reshape(depth, 1, v.shape[-1])
    bias, br, bi, lam = row(conv_b_bias), row(b_r), row(b_i), row(lru_lambda)
    g_pre_mix, g_post_mix = row(norm_pre_mix), row(norm_post_mix)
    g_pre_ffn, g_post_ffn = row(norm_pre_ffn), row(norm_post_ffn)
    sa = state_conv_a.reshape(depth, nb, (CONV_A_W - 1) * D_A)
    sb = state_conv_b.reshape(depth, nb, (CONV_B_W - 1) * D_B)

    xp = x_prompt.reshape(batch * seq, D_MODEL)
    xs = x_sample.transpose(1, 0, 2).reshape(nsteps * nb, D_MODEL)

    pa, pb, ph, s_a, s_b, s_h = [], [], [], [], [], []
    for l in range(depth):
        mix_w = (w16[0], conv_a_w, conv_b_w, bias, wri16, br, bi, lam, l)
        ya_p, yb_p, sga_p, sgb_p, na_p, nb_p, hl_p, *cast = _proj_mix(
            xp, g_pre_mix, *mix_w, seq=seq, cast_same=big_w[1:] if l == 0 else None)
        w16 = w16 + tuple(cast)
        w_in16, w_gu16, w_down16, w_out_a16, w_out_b16, w_o16 = w16
        ya_s, yb_s, sga_s, sgb_s, na_s, nb_s, hl_s = _proj_mix(
            xs, g_pre_mix, *mix_w, states=(sa, sb, state_lru_h, nb, nsteps))
        out_w = (w_out_a16, w_out_b16, w_o16)
        xp = _merge(ya_p, yb_p, sga_p, sgb_p, *out_w, xp, g_post_mix, l)
        xs = _merge(ya_s, yb_s, sga_s, sgb_s, *out_w, xs, g_post_mix, l)
        xp, w16 = _mlp(xp, g_pre_ffn, w_gu16, w_down16, g_post_ffn, l,
                       cast_next=big_w if l + 1 < depth else None)
        xs, _ = _mlp(xs, g_pre_ffn, w_gu16, w_down16, g_post_ffn, l)
        last = slice(seq // TM - 1, None, seq // TM)
        pa.append(na_p[last, SUBLANES - (CONV_A_W - 1):, :])
        pb.append(nb_p[last, SUBLANES - (CONV_B_W - 1):, :])
        ph.append(hl_p[last, 0, :])
        s_a.append(na_s)
        s_b.append(nb_s)
        s_h.append(hl_s)

    y_prompt = xp.reshape(batch, seq, D_MODEL)
    y_sample = xs.reshape(nsteps, nb, D_MODEL).transpose(1, 0, 2)
    return (y_prompt, y_sample, jnp.stack(pa), jnp.stack(pb), jnp.stack(ph),
            jnp.stack(s_a).transpose(0, 2, 1, 3), jnp.stack(s_b).transpose(0, 2, 1, 3),
            jnp.stack(s_h))
```

```python
import functools

import jax
import jax.numpy as jnp
from jax import lax
from jax.experimental import pallas as pl
from jax.experimental.pallas import tpu as pltpu

F32 = jnp.float32
BF16 = jnp.bfloat16

D_MODEL = 2048
D_A = D_MODEL // 2
D_B = D_MODEL
CONV_A_W = 3
CONV_B_W = 4
HEAD = 128
C_LRU = 8.0
LOG2_E = 1.4426950408889634
FFN_HIDDEN = 5632
EPS = 1e-6
IN_COLS = 3 * D_A + 2 * D_B + 2 * D_MODEL

OFF_A_BG, OFF_A_CG, OFF_A_X = 0, D_A, 2 * D_A
OFF_B_X = 3 * D_A
OFF_B_GATE = OFF_B_X + D_B
OFF_GATE_A = OFF_B_GATE + D_B
OFF_GATE_B = OFF_GATE_A + D_MODEL

SUBLANES = 8
BF16_ROWS = 16
LANES = 128
TM = 512
TC_MERGE = 512
TM_RESIDENT = 256
TH_MLP = 1408
MIX_BLOCKS = 4
CA = D_A // MIX_BLOCKS
CB = D_B // MIX_BLOCKS
VMEM_LIMIT = 56 * 1024 * 1024


def _params(n_axes):
    return pltpu.CompilerParams(
        dimension_semantics=("arbitrary",) * n_axes,
        vmem_limit_bytes=VMEM_LIMIT)


def _rms(x, g):
    var = jnp.mean(x * x, axis=-1, keepdims=True)
    return x * lax.rsqrt(var + EPS) * g


def _sigmoid(x):
    return 0.5 * jnp.tanh(0.5 * x) + 0.5


def _log_sigmoid(x):
    return jnp.minimum(x, 0.0) - jnp.log1p(jnp.exp(-jnp.abs(x)))


def _head(h):
    return slice(h * HEAD, (h + 1) * HEAD)


def _gate_pre(xb, wri_ref, h):
    return jnp.dot(xb[:, _head(h)].astype(BF16), wri_ref[h], preferred_element_type=F32)


def _lru_coeffs_head(xb, pre, b_r, b_i, lam, a_ref, b_ref, h):
    sl = _head(h)
    r = _sigmoid(pre[:, :HEAD] + b_r[:, sl])
    i = _sigmoid(pre[:, HEAD:] + b_i[:, sl])
    a = jnp.exp2(r * ((C_LRU * LOG2_E) * _log_sigmoid(lam[:, sl])))
    y = 1.0 - a * a
    mult = jnp.where(y > 0.0, y * lax.rsqrt(y), 0.0)
    a_ref[:, sl] = a
    b_ref[:, sl] = mult * i * xb[:, sl]


def _lru_coeffs(xb, wri_ref, b_r, b_i, lam, a_ref, b_ref):
    for h in range(CB // HEAD):
        _lru_coeffs_head(xb, _gate_pre(xb, wri_ref, h), b_r, b_i, lam, a_ref, b_ref, h)


def _scan_groups(a_ref, b_ref, h_ref, h_in, n_groups):
    row = lax.broadcasted_iota(jnp.int32, (SUBLANES, CB), 0)

    h_prev = h_in
    for g in range(n_groups):
        rows = slice(g * SUBLANES, (g + 1) * SUBLANES)
        a = a_ref[rows, :]
        b = b_ref[rows, :]
        for s in (1, 2, 4):
            keep = row >= s
            a_sh = jnp.where(keep, pltpu.roll(a, s, 0), 1.0)
            b_sh = jnp.where(keep, pltpu.roll(b, s, 0), 0.0)
            b = a * b_sh + b
            a = a * a_sh
        h = a * h_prev + b
        h_ref[rows, :] = h
        h_prev = jnp.broadcast_to(h[SUBLANES - 1:SUBLANES, :], (SUBLANES, CB))
    return h_prev


def _norm(x_ref, g_ref, u_ref):
    @pl.when(pl.program_id(1) == 0)
    def _():
        u_ref[...] = _rms(x_ref[...], g_ref[...]).astype(BF16)

    return u_ref[...]


def _gates(u, wga_ref, wgb_ref, sga_ref, sgb_ref):
    sga_ref[...] = _sigmoid(
        jnp.dot(u, wga_ref[...], preferred_element_type=F32)).astype(BF16)
    sgb_ref[...] = _sigmoid(
        jnp.dot(u, wgb_ref[...], preferred_element_type=F32)).astype(BF16)


def _cast_operands(chunks, src_layer):
    in_specs, out_specs, out_shape, args = [], [], [], []
    for w, row_blocks, col_blocks, idx in chunks:
        rows, cols = w.shape[1] // row_blocks, w.shape[2] // col_blocks
        assert (rows * row_blocks, cols * col_blocks) == w.shape[1:], w.shape
        assert rows % BF16_ROWS == 0 and cols % LANES == 0, (w.shape, rows, cols)
        in_specs.append(pl.BlockSpec(
            (None, rows, cols), lambda i, j, idx=idx: (src_layer,) + idx(i, j)))
        out_specs.append(pl.BlockSpec(
            (None, rows, cols), lambda i, j, idx=idx: (0,) + idx(i, j)))
        out_shape.append(jax.ShapeDtypeStruct((1,) + w.shape[1:], BF16))
        args.append(w)
    return in_specs, out_specs, out_shape, args


def _cast_blocks(cast_in, cast_out):
    for src, dst in zip(cast_in, cast_out):
        dst[...] = src[...].astype(BF16)


PROJ_MIX_INPUTS = 16
PROJ_MIX_OUTPUTS = 7


def _proj_mix_prompt_kernel(tiles_per_seq, n_cast, *refs):
    (x_ref, g_ref, wabg_ref, wacg_ref, wax_ref, wbx_ref, wbg_ref, wga_ref, wgb_ref,
     wa_ref, wb_ref, bias_ref, wri_ref, br_ref, bi_ref, lam_ref) = refs[:PROJ_MIX_INPUTS]
    refs = refs[PROJ_MIX_INPUTS:]
    cast_in, refs = refs[:n_cast], refs[n_cast:]
    ya_ref, yb_ref, sga_ref, sgb_ref, na_ref, nb_ref, hl_ref = refs[:PROJ_MIX_OUTPUTS]
    refs = refs[PROJ_MIX_OUTPUTS:]
    cast_out, refs = refs[:n_cast], refs[n_cast:]
    (u_ref, exta_ref, extb_ref, taila_ref, tailb_ref, hc_ref, a_buf, b_buf, h_buf) = refs
    c = pl.program_id(1)
    u = _norm(x_ref, g_ref, u_ref)
    dot = lambda w_ref: jnp.dot(u, w_ref[...], preferred_element_type=F32)

    @pl.when(pl.program_id(0) % tiles_per_seq == 0)
    def _():
        taila_ref[c] = jnp.zeros((SUBLANES, CA), F32)
        tailb_ref[c] = jnp.zeros((SUBLANES, CB), F32)
        hc_ref[c] = jnp.zeros((SUBLANES, CB), F32)

    pre_ga = dot(wga_ref)
    _cast_blocks(cast_in, cast_out)
    bx = dot(wbx_ref)
    sga_ref[...] = _sigmoid(pre_ga).astype(BF16)

    pre_gb = dot(wgb_ref)
    extb_ref[0:SUBLANES, :] = tailb_ref[c]
    extb_ref[SUBLANES:, :] = bx
    wb = wb_ref[...]
    xb = (extb_ref[SUBLANES - 3:SUBLANES - 3 + TM, :] * wb[0:1]
          + extb_ref[SUBLANES - 2:SUBLANES - 2 + TM, :] * wb[1:2]
          + extb_ref[SUBLANES - 1:SUBLANES - 1 + TM, :] * wb[2:3]
          + bx * wb[3:4]) + bias_ref[...]
    tail_b = extb_ref[TM:TM + SUBLANES, :]
    nb_ref[...] = tail_b
    tailb_ref[c] = tail_b

    heads = range(CB // HEAD)
    pres = [_gate_pre(xb, wri_ref, h) for h in heads]
    acg = dot(wacg_ref)
    sgb_ref[...] = _sigmoid(pre_gb).astype(BF16)
    coeffs = functools.partial(_lru_coeffs_head, xb, b_r=br_ref[...], b_i=bi_ref[...],
                               lam=lam_ref[...], a_ref=a_buf, b_ref=b_buf)
    ax = dot(wax_ref)
    coeffs(pre=pres[0], h=0)
    bg = dot(wbg_ref)
    coeffs(pre=pres[1], h=1)
    coeffs(pre=pres[2], h=2)

    z = acg * ax
    exta_ref[0:SUBLANES, :] = taila_ref[c]
    exta_ref[SUBLANES:, :] = z
    wa = wa_ref[...]
    conv = (exta_ref[SUBLANES - 2:SUBLANES - 2 + TM, :] * wa[0:1]
            + exta_ref[SUBLANES - 1:SUBLANES - 1 + TM, :] * wa[1:2]
            + z * wa[2:3])
    tail_a = exta_ref[TM:TM + SUBLANES, :]
    na_ref[...] = tail_a
    taila_ref[c] = tail_a

    coeffs(pre=pres[3], h=3)
    abg = dot(wabg_ref)
    h_out = _scan_groups(a_buf, b_buf, h_buf, hc_ref[c], TM // SUBLANES)
    hc_ref[c] = h_out
    hl_ref[...] = h_out
    yb_ref[...] = (h_buf[...] * jax.nn.gelu(bg)).astype(BF16)
    ya_ref[...] = (abg * conv).astype(BF16)


def _proj_mix_sample_kernel(nb, nsteps, x_ref, g_ref,
                            wabg_ref, wacg_ref, wax_ref, wbx_ref, wbg_ref, wga_ref, wgb_ref,
                            wa_ref, wb_ref, bias_ref, wri_ref, br_ref, bi_ref, lam_ref,
                            sa0_ref, sa1_ref, sb0_ref, sb1_ref, sb2_ref, h0_ref,
                            ya_ref, yb_ref, sga_ref, sgb_ref, na_ref, nb_ref, hl_ref,
                            u_ref, a_buf, b_buf):
    u = _norm(x_ref, g_ref, u_ref)
    _gates(u, wga_ref, wgb_ref, sga_ref, sgb_ref)
    dot = lambda w_ref: jnp.dot(u, w_ref[...], preferred_element_type=F32)
    slab = lambda t: slice(t * nb, (t + 1) * nb)

    z = dot(wacg_ref) * dot(wax_ref)
    abg = dot(wabg_ref)
    ins = [sa0_ref[...], sa1_ref[...]] + [z[slab(t), :] for t in range(nsteps)]
    wa = wa_ref[...]
    for t in range(nsteps):
        conv = ins[t] * wa[0:1] + ins[t + 1] * wa[1:2] + ins[t + 2] * wa[2:3]
        ya_ref[slab(t), :] = (abg[slab(t), :] * conv).astype(BF16)
    for k in range(CONV_A_W - 1):
        na_ref[k] = ins[nsteps + k]

    bx = dot(wbx_ref)
    ins = ([sb0_ref[...], sb1_ref[...], sb2_ref[...]]
           + [bx[slab(t), :] for t in range(nsteps)])
    wb = wb_ref[...]
    xb = jnp.concatenate(
        [(ins[t] * wb[0:1] + ins[t + 1] * wb[1:2] + ins[t + 2] * wb[2:3]
          + ins[t + 3] * wb[3:4]) + bias_ref[...] for t in range(nsteps)], axis=0)
    for k in range(CONV_B_W - 1):
        nb_ref[k] = ins[nsteps + k]

    _lru_coeffs(xb, wri_ref, br_ref[...], bi_ref[...], lam_ref[...], a_buf, b_buf)
    gate = jax.nn.gelu(dot(wbg_ref))
    h = h0_ref[...]
    for t in range(nsteps):
        h = a_buf[slab(t), :] * h + b_buf[slab(t), :]
        yb_ref[slab(t), :] = (h * gate[slab(t), :]).astype(BF16)
    hl_ref[...] = h


def _proj_mix(x, g, w_in16, wa, wb, bias, wri, br, bi, lam, layer, *,
              seq=None, states=None, cast_same=None):
    m = x.shape[0]
    n_i = m // TM
    w_col = lambda width, off: pl.BlockSpec(
        (None, D_MODEL, width), lambda i, c: (0, 0, off // width + c))
    vec_b = pl.BlockSpec((None, 1, CB), lambda i, c: (layer, 0, c))
    in_specs = [
        pl.BlockSpec((TM, D_MODEL), lambda i, c: (i, 0)),
        pl.BlockSpec((None, 1, D_MODEL), lambda i, c: (layer, 0, 0)),
        w_col(CA, OFF_A_BG), w_col(CA, OFF_A_CG), w_col(CA, OFF_A_X),
        w_col(CB, OFF_B_X), w_col(CB, OFF_B_GATE),
        w_col(CB, OFF_GATE_A), w_col(CB, OFF_GATE_B),
        pl.BlockSpec((None, CONV_A_W, CA), lambda i, c: (layer, 0, c)),
        pl.BlockSpec((None, CONV_B_W, CB), lambda i, c: (layer, 0, c)),
        vec_b,
        pl.BlockSpec((None, CB // HEAD, HEAD, 2 * HEAD), lambda i, c: (layer, c, 0, 0)),
        vec_b, vec_b, vec_b,
    ]
    args = [x, g] + [w_in16] * 7 + [wa, wb, bias, wri, br, bi, lam]
    out_specs = [
        pl.BlockSpec((TM, CA), lambda i, c: (i, c)),
        pl.BlockSpec((TM, CB), lambda i, c: (i, c)),
        pl.BlockSpec((TM, CB), lambda i, c: (i, c)),
        pl.BlockSpec((TM, CB), lambda i, c: (i, c)),
    ]
    out_shape = [
        jax.ShapeDtypeStruct((m, D_A), BF16),
        jax.ShapeDtypeStruct((m, D_B), BF16),
        jax.ShapeDtypeStruct((m, D_MODEL), BF16),
        jax.ShapeDtypeStruct((m, D_MODEL), BF16),
    ]
    scratch = [pltpu.VMEM((TM, D_MODEL), BF16)]
    if states is None:
        tiles_per_seq = seq // TM
        n_cast = 0
        if cast_same is not None:
            chunks = [(w, n_i, MIX_BLOCKS, lambda i, c: (i, c)) for w in cast_same]
            n_cast = len(chunks)
            c_in, c_out, c_shape, c_args = _cast_operands(chunks, layer)
        body = functools.partial(_proj_mix_prompt_kernel, tiles_per_seq, n_cast)
        out_specs += [
            pl.BlockSpec((None, SUBLANES, CA), lambda i, c: (i, 0, c)),
            pl.BlockSpec((None, SUBLANES, CB), lambda i, c: (i, 0, c)),
            pl.BlockSpec((None, SUBLANES, CB), lambda i, c: (i, 0, c)),
        ]
        out_shape += [
            jax.ShapeDtypeStruct((m // TM, SUBLANES, D_A), F32),
            jax.ShapeDtypeStruct((m // TM, SUBLANES, D_B), F32),
            jax.ShapeDtypeStruct((m // TM, SUBLANES, D_B), F32),
        ]
        scratch += [
            pltpu.VMEM((SUBLANES + TM, CA), F32),
            pltpu.VMEM((SUBLANES + TM, CB), F32),
            pltpu.VMEM((MIX_BLOCKS, SUBLANES, CA), F32),
            pltpu.VMEM((MIX_BLOCKS, SUBLANES, CB), F32),
            pltpu.VMEM((MIX_BLOCKS, SUBLANES, CB), F32),
            pltpu.VMEM((TM, CB), F32),
            pltpu.VMEM((TM, CB), F32),
            pltpu.VMEM((TM, CB), F32),
        ]
        assert (len(in_specs), len(out_specs)) == (PROJ_MIX_INPUTS, PROJ_MIX_OUTPUTS)
        if n_cast:
            in_specs, args = in_specs + c_in, args + c_args
            out_specs, out_shape = out_specs + c_out, out_shape + c_shape
        name = "proj_mix_prompt_cast" if n_cast else "proj_mix_prompt"
    else:
        sa, sb, h0, nb, nsteps = states
        assert m == TM == nb * nsteps
        body = functools.partial(_proj_mix_sample_kernel, nb, nsteps)
        st_a = lambda k: pl.BlockSpec(
            (None, nb, CA), lambda i, c: (layer, 0, k * MIX_BLOCKS + c))
        st_b = lambda k: pl.BlockSpec(
            (None, nb, CB), lambda i, c: (layer, 0, k * MIX_BLOCKS + c))
        in_specs += [st_a(0), st_a(1), st_b(0), st_b(1), st_b(2),
                     pl.BlockSpec((None, nb, CB), lambda i, c: (layer, 0, c))]
        args += [sa, sa, sb, sb, sb, h0]
        out_specs += [
            pl.BlockSpec((CONV_A_W - 1, nb, CA), lambda i, c: (0, 0, c)),
            pl.BlockSpec((CONV_B_W - 1, nb, CB), lambda i, c: (0, 0, c)),
            pl.BlockSpec((nb, CB), lambda i, c: (0, c)),
        ]
        out_shape += [
            jax.ShapeDtypeStruct((CONV_A_W - 1, nb, D_A), F32),
            jax.ShapeDtypeStruct((CONV_B_W - 1, nb, D_B), F32),
            jax.ShapeDtypeStruct((nb, D_B), F32),
        ]
        scratch += [pltpu.VMEM((TM, CB), F32), pltpu.VMEM((TM, CB), F32)]
        name = "proj_mix_sample"
    return pl.pallas_call(
        body,
        grid=(m // TM, MIX_BLOCKS),
        in_specs=in_specs,
        out_specs=out_specs,
        out_shape=out_shape,
        scratch_shapes=scratch,
        compiler_params=_params(2),
        name=name,
    )(*args)


def _merge_kernel(ya_ref, yb_ref, sga_ref, sgb_ref, woa_ref, wob_ref, wo_ref,
                  x_ref, g_ref, gffn_ref, o_ref, v_ref):
    tc = TC_MERGE
    ya, yb = ya_ref[...], yb_ref[...]
    mix = None
    for jb in range(D_MODEL // tc):
        cols = slice(jb * tc, (jb + 1) * tc)
        y_a = jnp.dot(ya, woa_ref[:, cols], preferred_element_type=F32)
        y_b = jnp.dot(yb, wob_ref[:, cols], preferred_element_type=F32)
        merged = (sga_ref[:, cols].astype(F32) * y_a
                  + sgb_ref[:, cols].astype(F32) * y_b).astype(BF16)
        part = jnp.dot(merged, wo_ref[cols, :], preferred_element_type=F32)
        mix = part if mix is None else mix + part
    x_out = x_ref[...] + _rms(mix, g_ref[...])
    o_ref[...] = x_out
    v_ref[...] = _rms(x_out, gffn_ref[...]).astype(BF16)


def _merge(ya, yb, sga, sgb, woa, wob, wo, x, g, g_ffn, layer):
    m = x.shape[0]
    tm = TM_RESIDENT
    rows = lambda width: pl.BlockSpec((tm, width), lambda i: (i, 0))
    resident = lambda k, n: pl.BlockSpec(
        (None, k, n), lambda i: (0, 0, 0), pipeline_mode=pl.Buffered(1))
    return pl.pallas_call(
        _merge_kernel,
        grid=(m // tm,),
        in_specs=[
            rows(D_A), rows(D_B), rows(D_MODEL), rows(D_MODEL),
            resident(D_A, D_MODEL), resident(D_B, D_MODEL), resident(D_MODEL, D_MODEL),
            rows(D_MODEL),
            pl.BlockSpec((None, 1, D_MODEL), lambda i: (layer, 0, 0)),
            pl.BlockSpec((None, 1, D_MODEL), lambda i: (layer, 0, 0)),
        ],
        out_specs=[rows(D_MODEL), rows(D_MODEL)],
        out_shape=[jax.ShapeDtypeStruct((m, D_MODEL), F32),
                   jax.ShapeDtypeStruct((m, D_MODEL), BF16)],
        compiler_params=_params(1),
        name="merge",
    )(ya, yb, sga, sgb, woa, wob, wo, x, g, g_ffn)


def _mlp_up_kernel(n_cast, v_ref, wg_ref, wu_ref, *refs):
    cast_in, h_ref, cast_out = refs[:n_cast], refs[n_cast], refs[n_cast + 1:]
    v = v_ref[...]
    g = jnp.dot(v, wg_ref[...], preferred_element_type=F32)
    _cast_blocks(cast_in, cast_out)
    u = jnp.dot(v, wu_ref[...], preferred_element_type=F32)
    h_ref[...] = ((g * _sigmoid(g)) * u).astype(BF16)


def _mlp_up(v, wgu, layer, cast_next=None):
    m = v.shape[0]
    nh = FFN_HIDDEN // TH_MLP
    n_i = m // TM
    in_specs = [
        pl.BlockSpec((TM, D_MODEL), lambda j, i: (i, 0)),
        pl.BlockSpec((None, D_MODEL, TH_MLP), lambda j, i: (0, 0, j)),
        pl.BlockSpec((None, D_MODEL, TH_MLP), lambda j, i: (0, 0, nh + j)),
    ]
    out_specs = [pl.BlockSpec((TM, TH_MLP), lambda j, i: (i, j))]
    out_shape = [jax.ShapeDtypeStruct((m, FFN_HIDDEN), BF16)]
    args = [v, wgu, wgu]
    n_cast = 0
    if cast_next is not None:
        w_in, w_gu, w_down, w_oa, w_ob, w_o = cast_next
        chunks = [
            (w_in, n_i, nh, lambda j, i: (i, j)),
            (w_gu, n_i, nh, lambda j, i: (i, j)),
            (w_down, nh, n_i, lambda j, i: (j, i)),
            (w_oa, n_i, nh, lambda j, i: (i, j)),
            (w_ob, n_i, nh, lambda j, i: (i, j)),
            (w_o, n_i, nh, lambda j, i: (i, j)),
        ]
        n_cast = len(chunks)
        c_in, c_out, c_shape, c_args = _cast_operands(chunks, layer + 1)
        in_specs, args = in_specs + c_in, args + c_args
        out_specs, out_shape = out_specs + c_out, out_shape + c_shape
    outs = pl.pallas_call(
        functools.partial(_mlp_up_kernel, n_cast),
        grid=(nh, n_i),
        in_specs=in_specs,
        out_specs=out_specs,
        out_shape=out_shape,
        compiler_params=_params(2),
        name="mlp_up_cast" if n_cast else "mlp_up",
    )(*args)
    return outs[0], tuple(outs[1:])


def _mlp_down_kernel(h_ref, wd_ref, x_ref, g_ref, o_ref):
    f = jnp.dot(h_ref[...], wd_ref[...], preferred_element_type=F32)
    o_ref[...] = x_ref[...] + _rms(f, g_ref[...])


def _mlp_down(hidden, wd, x, gpost, layer):
    m = x.shape[0]
    tm = TM_RESIDENT
    rows = lambda width: pl.BlockSpec((tm, width), lambda i: (i, 0))
    return pl.pallas_call(
        _mlp_down_kernel,
        grid=(m // tm,),
        in_specs=[
            rows(FFN_HIDDEN),
            pl.BlockSpec((None, FFN_HIDDEN, D_MODEL), lambda i: (0, 0, 0),
                         pipeline_mode=pl.Buffered(1)),
            rows(D_MODEL),
            pl.BlockSpec((None, 1, D_MODEL), lambda i: (layer, 0, 0)),
        ],
        out_specs=rows(D_MODEL),
        out_shape=jax.ShapeDtypeStruct((m, D_MODEL), F32),
        compiler_params=_params(1),
        name="mlp_down",
    )(hidden, wd, x, gpost)


def kernel(x_prompt, x_sample, state_conv_a, state_conv_b, state_lru_h, w_in, conv_a_w,
           w_out_a, conv_b_w, conv_b_bias, w_r, b_r, w_i, b_i, lru_lambda, w_out_b, w_o,
           norm_pre_mix, norm_post_mix, norm_pre_ffn, norm_post_ffn, w_gate_up, w_down):
    batch, seq, _ = x_prompt.shape
    nb, nsteps, _ = x_sample.shape
    depth = w_in.shape[0]
    assert seq % TM == 0 and nb * nsteps == TM
    assert nsteps >= CONV_B_W - 1

    big_w = (w_in, w_gate_up, w_down, w_out_a, w_out_b, w_o)
    w16 = (w_in[:1].astype(BF16),)
    wri16 = jnp.concatenate([w_r, w_i], axis=-1).astype(BF16)
    row = lambda v: v.reshape(depth, 1, v.shape[-1])
    bias, br, bi, lam = row(conv_b_bias), row(b_r), row(b_i), row(lru_lambda)
    g_pre_mix, g_post_mix = row(norm_pre_mix), row(norm_post_mix)
    g_pre_ffn, g_post_ffn = row(norm_pre_ffn), row(norm_post_ffn)
    sa = state_conv_a.reshape(depth, nb, (CONV_A_W - 1) * D_A)
    sb = state_conv_b.reshape(depth, nb, (CONV_B_W - 1) * D_B)

    xp = x_prompt.reshape(batch * seq, D_MODEL)
    xs = x_sample.transpose(1, 0, 2).reshape(nsteps * nb, D_MODEL)

    pa, pb, ph, s_a, s_b, s_h = [], [], [], [], [], []
    for l in range(depth):
        mix_w = (w16[0], conv_a_w, conv_b_w, bias, wri16, br, bi, lam, l)
        ya_p, yb_p, sga_p, sgb_p, na_p, nb_p, hl_p, *cast = _proj_mix(
            xp, g_pre_mix, *mix_w, seq=seq, cast_same=big_w[1:] if l == 0 else None)
        w16 = w16 + tuple(cast)
        w_in16, w_gu16, w_down16, w_out_a16, w_out_b16, w_o16 = w16
        ya_s, yb_s, sga_s, sgb_s, na_s, nb_s, hl_s = _proj_mix(
            xs, g_pre_mix, *mix_w, states=(sa, sb, state_lru_h, nb, nsteps))
        out_w = (w_out_a16, w_out_b16, w_o16)
        xp, vp = _merge(ya_p, yb_p, sga_p, sgb_p, *out_w, xp, g_post_mix, g_pre_ffn, l)
        xs, vs = _merge(ya_s, yb_s, sga_s, sgb_s, *out_w, xs, g_post_mix, g_pre_ffn, l)
        hp, w16 = _mlp_up(vp, w_gu16, l, cast_next=big_w if l + 1 < depth else None)
        hs, _ = _mlp_up(vs, w_gu16, l)
        xp = _mlp_down(hp, w_down16, xp, g_post_ffn, l)
        xs = _mlp_down(hs, w_down16, xs, g_post_ffn, l)
        last = slice(seq // TM - 1, None, seq // TM)
        pa.append(na_p[last, SUBLANES - (CONV_A_W - 1):, :])
        pb.append(nb_p[last, SUBLANES - (CONV_B_W - 1):, :])
        ph.append(hl_p[last, 0, :])
        s_a.append(na_s)
        s_b.append(nb_s)
        s_h.append(hl_s)

    y_prompt = xp.reshape(batch, seq, D_MODEL)
    y_sample = xs.reshape(nsteps, nb, D_MODEL).transpose(1, 0, 2)
    return (y_prompt, y_sample, jnp.stack(pa), jnp.stack(pb), jnp.stack(ph),
            jnp.stack(s_a).transpose(0, 2, 1, 3), jnp.stack(s_b).transpose(0, 2, 1, 3),
            jnp.stack(s_h))
```

```python
import functools

import jax
import jax.numpy as jnp
from jax import lax
from jax.experimental import pallas as pl
from jax.experimental.pallas import tpu as pltpu

F32 = jnp.float32
BF16 = jnp.bfloat16

D_MODEL = 2048
D_A = D_MODEL // 2
D_B = D_MODEL
CONV_A_W = 3
CONV_B_W = 4
HEAD = 128
C_LRU = 8.0
LOG2_E = 1.4426950408889634
FFN_HIDDEN = 5632
EPS = 1e-6
IN_COLS = 3 * D_A + 2 * D_B + 2 * D_MODEL

OFF_A_BG, OFF_A_CG, OFF_A_X = 0, D_A, 2 * D_A
OFF_B_X = 3 * D_A
OFF_B_GATE = OFF_B_X + D_B
OFF_GATE_A = OFF_B_GATE + D_B
OFF_GATE_B = OFF_GATE_A + D_MODEL

SUBLANES = 8
BF16_ROWS = 16
LANES = 128
TM = 512
TM_MIX = 256
TC_MERGE = 512
TM_RESIDENT = 256
TH_MLP = 1408
MIX_BLOCKS = 4
CA = D_A // MIX_BLOCKS
CB = D_B // MIX_BLOCKS
VMEM_LIMIT = 56 * 1024 * 1024


def _params(n_axes):
    return pltpu.CompilerParams(
        dimension_semantics=("arbitrary",) * n_axes,
        vmem_limit_bytes=VMEM_LIMIT)


def _rms(x, g):
    var = jnp.mean(x * x, axis=-1, keepdims=True)
    return x * lax.rsqrt(var + EPS) * g


def _sigmoid(x):
    return 0.5 * jnp.tanh(0.5 * x) + 0.5


def _log_sigmoid(x):
    return jnp.minimum(x, 0.0) - jnp.log1p(jnp.exp(-jnp.abs(x)))


def _head(h):
    return slice(h * HEAD, (h + 1) * HEAD)


def _gate_pre(xb, wri_ref, h):
    return jnp.dot(xb[:, _head(h)].astype(BF16), wri_ref[h], preferred_element_type=F32)


def _lru_coeffs_head(xb, pre, b_r, b_i, lam, a_ref, b_ref, h):
    sl = _head(h)
    r = _sigmoid(pre[:, :HEAD] + b_r[:, sl])
    i = _sigmoid(pre[:, HEAD:] + b_i[:, sl])
    a = jnp.exp2(r * ((C_LRU * LOG2_E) * _log_sigmoid(lam[:, sl])))
    y = 1.0 - a * a
    mult = jnp.where(y > 0.0, y * lax.rsqrt(y), 0.0)
    a_ref[:, sl] = a
    b_ref[:, sl] = mult * i * xb[:, sl]


def _lru_coeffs(xb, wri_ref, b_r, b_i, lam, a_ref, b_ref):
    for h in range(CB // HEAD):
        _lru_coeffs_head(xb, _gate_pre(xb, wri_ref, h), b_r, b_i, lam, a_ref, b_ref, h)


def _scan_groups(a_ref, b_ref, h_ref, h_in, n_groups):
    row = lax.broadcasted_iota(jnp.int32, (SUBLANES, CB), 0)

    h_prev = h_in
    for g in range(n_groups):
        rows = slice(g * SUBLANES, (g + 1) * SUBLANES)
        a = a_ref[rows, :]
        b = b_ref[rows, :]
        for s in (1, 2, 4):
            keep = row >= s
            a_sh = jnp.where(keep, pltpu.roll(a, s, 0), 1.0)
            b_sh = jnp.where(keep, pltpu.roll(b, s, 0), 0.0)
            b = a * b_sh + b
            a = a * a_sh
        h = a * h_prev + b
        h_ref[rows, :] = h
        h_prev = jnp.broadcast_to(h[SUBLANES - 1:SUBLANES, :], (SUBLANES, CB))
    return h_prev


def _norm(x_ref, g_ref, u_ref):
    @pl.when(pl.program_id(1) == 0)
    def _():
        u_ref[...] = _rms(x_ref[...], g_ref[...]).astype(BF16)

    return u_ref[...]


def _gates(u, wga_ref, wgb_ref, sga_ref, sgb_ref):
    sga_ref[...] = _sigmoid(
        jnp.dot(u, wga_ref[...], preferred_element_type=F32)).astype(BF16)
    sgb_ref[...] = _sigmoid(
        jnp.dot(u, wgb_ref[...], preferred_element_type=F32)).astype(BF16)


def _cast_operands(chunks, src_layer):
    in_specs, out_specs, out_shape, args = [], [], [], []
    for w, row_blocks, col_blocks, idx in chunks:
        rows, cols = w.shape[1] // row_blocks, w.shape[2] // col_blocks
        assert (rows * row_blocks, cols * col_blocks) == w.shape[1:], w.shape
        assert rows % BF16_ROWS == 0 and cols % LANES == 0, (w.shape, rows, cols)
        in_specs.append(pl.BlockSpec(
            (None, rows, cols), lambda *g, idx=idx: (src_layer,) + idx(*g)))
        out_specs.append(pl.BlockSpec(
            (None, rows, cols), lambda *g, idx=idx: (0,) + idx(*g)))
        out_shape.append(jax.ShapeDtypeStruct((1,) + w.shape[1:], BF16))
        args.append(w)
    return in_specs, out_specs, out_shape, args


def _cast_blocks(cast_in, cast_out):
    for src, dst in zip(cast_in, cast_out):
        dst[...] = src[...].astype(BF16)


PROJ_MIX_INPUTS = 16
PROJ_MIX_OUTPUTS = 7
N_W_IN_SLICES = 7
P_ABG, P_ACG, P_AX = 0, CA, 2 * CA
P_BX = 3 * CA
P_BG, P_GA, P_GB = P_BX + CB, P_BX + 2 * CB, P_BX + 3 * CB
P_COLS = P_BX + 4 * CB


def _proj_mix_prompt_kernel(tiles_per_seq, n_blocks, n_cast, *refs):
    (x_ref, g_ref, wabg_ref, wacg_ref, wax_ref, wbx_ref, wbg_ref, wga_ref, wgb_ref,
     wa_ref, wb_ref, bias_ref, wri_ref, br_ref, bi_ref, lam_ref) = refs[:PROJ_MIX_INPUTS]
    refs = refs[PROJ_MIX_INPUTS:]
    cast_in, refs = refs[:n_cast], refs[n_cast:]
    ya_ref, yb_ref, sga_ref, sgb_ref, na_ref, nb_ref, hl_ref = refs[:PROJ_MIX_OUTPUTS]
    refs = refs[PROJ_MIX_OUTPUTS:]
    cast_out, refs = refs[:n_cast], refs[n_cast:]
    (u_ref, stage0_ref, stage1_ref, exta_ref, extb_ref, taila_ref, tailb_ref, hc_ref,
     a_buf, b_buf, h_buf) = refs
    tm = TM_MIX
    s = pl.program_id(0)
    c_dot = jnp.minimum(s, n_blocks - 1) % MIX_BLOCKS
    mixed = jnp.maximum(s - 1, 0)
    c = mixed % MIX_BLOCKS

    @pl.when(jnp.logical_and(c_dot == 0, s < n_blocks))
    def _():
        u_ref[...] = _rms(x_ref[...], g_ref[...]).astype(BF16)

    @pl.when(s == 0)
    def _():
        stage1_ref[...] = jnp.zeros_like(stage1_ref)

    @pl.when((mixed // MIX_BLOCKS) % tiles_per_seq == 0)
    def _():
        taila_ref[c] = jnp.zeros((SUBLANES, CA), F32)
        tailb_ref[c] = jnp.zeros((SUBLANES, CB), F32)
        hc_ref[c] = jnp.zeros((SUBLANES, CB), F32)

    def step(new_ref, old_ref):
        u = u_ref[...]

        def dot_into(w_ref, col, width):
            new_ref[:, col:col + width] = jnp.dot(
                u, w_ref[...], preferred_element_type=F32)

        dot_into(wga_ref, P_GA, CB)
        _cast_blocks(cast_in, cast_out)
        sga_ref[...] = _sigmoid(old_ref[:, P_GA:P_GA + CB]).astype(BF16)

        dot_into(wbx_ref, P_BX, CB)
        bx = old_ref[:, P_BX:P_BX + CB]
        extb_ref[0:SUBLANES, :] = tailb_ref[c]
        extb_ref[SUBLANES:, :] = bx
        wb = wb_ref[...]
        xb = (extb_ref[SUBLANES - 3:SUBLANES - 3 + tm, :] * wb[0:1]
              + extb_ref[SUBLANES - 2:SUBLANES - 2 + tm, :] * wb[1:2]
              + extb_ref[SUBLANES - 1:SUBLANES - 1 + tm, :] * wb[2:3]
              + bx * wb[3:4]) + bias_ref[...]
        tail_b = extb_ref[tm:tm + SUBLANES, :]
        nb_ref[...] = tail_b
        tailb_ref[c] = tail_b

        dot_into(wgb_ref, P_GB, CB)
        heads = range(CB // HEAD)
        pres = [_gate_pre(xb, wri_ref, h) for h in heads]
        sgb_ref[...] = _sigmoid(old_ref[:, P_GB:P_GB + CB]).astype(BF16)
        coeffs = functools.partial(_lru_coeffs_head, xb, b_r=br_ref[...], b_i=bi_ref[...],
                                   lam=lam_ref[...], a_ref=a_buf, b_ref=b_buf)
        dot_into(wacg_ref, P_ACG, CA)
        coeffs(pre=pres[0], h=0)
        coeffs(pre=pres[1], h=1)
        dot_into(wax_ref, P_AX, CA)
        coeffs(pre=pres[2], h=2)
        coeffs(pre=pres[3], h=3)
        dot_into(wbg_ref, P_BG, CB)

        z = old_ref[:, P_ACG:P_ACG + CA] * old_ref[:, P_AX:P_AX + CA]
        exta_ref[0:SUBLANES, :] = taila_ref[c]
        exta_ref[SUBLANES:, :] = z
        wa = wa_ref[...]
        conv = (exta_ref[SUBLANES - 2:SUBLANES - 2 + tm, :] * wa[0:1]
                + exta_ref[SUBLANES - 1:SUBLANES - 1 + tm, :] * wa[1:2]
                + z * wa[2:3])
        tail_a = exta_ref[tm:tm + SUBLANES, :]
        na_ref[...] = tail_a
        taila_ref[c] = tail_a
        ya_ref[...] = (old_ref[:, P_ABG:P_ABG + CA] * conv).astype(BF16)

        dot_into(wabg_ref, P_ABG, CA)
        h_out = _scan_groups(a_buf, b_buf, h_buf, hc_ref[c], tm // SUBLANES)
        hc_ref[c] = h_out
        hl_ref[...] = h_out
        yb_ref[...] = (h_buf[...] * jax.nn.gelu(old_ref[:, P_BG:P_BG + CB])).astype(BF16)

    @pl.when(s % 2 == 0)
    def _():
        step(stage0_ref, stage1_ref)

    @pl.when(s % 2 == 1)
    def _():
        step(stage1_ref, stage0_ref)


def _proj_mix_sample_kernel(nb, nsteps, x_ref, g_ref,
                            wabg_ref, wacg_ref, wax_ref, wbx_ref, wbg_ref, wga_ref, wgb_ref,
                            wa_ref, wb_ref, bias_ref, wri_ref, br_ref, bi_ref, lam_ref,
                            sa0_ref, sa1_ref, sb0_ref, sb1_ref, sb2_ref, h0_ref,
                            ya_ref, yb_ref, sga_ref, sgb_ref, na_ref, nb_ref, hl_ref,
                            u_ref, a_buf, b_buf):
    u = _norm(x_ref, g_ref, u_ref)
    _gates(u, wga_ref, wgb_ref, sga_ref, sgb_ref)
    dot = lambda w_ref: jnp.dot(u, w_ref[...], preferred_element_type=F32)
    slab = lambda t: slice(t * nb, (t + 1) * nb)

    z = dot(wacg_ref) * dot(wax_ref)
    abg = dot(wabg_ref)
    ins = [sa0_ref[...], sa1_ref[...]] + [z[slab(t), :] for t in range(nsteps)]
    wa = wa_ref[...]
    for t in range(nsteps):
        conv = ins[t] * wa[0:1] + ins[t + 1] * wa[1:2] + ins[t + 2] * wa[2:3]
        ya_ref[slab(t), :] = (abg[slab(t), :] * conv).astype(BF16)
    for k in range(CONV_A_W - 1):
        na_ref[k] = ins[nsteps + k]

    bx = dot(wbx_ref)
    ins = ([sb0_ref[...], sb1_ref[...], sb2_ref[...]]
           + [bx[slab(t), :] for t in range(nsteps)])
    wb = wb_ref[...]
    xb = jnp.concatenate(
        [(ins[t] * wb[0:1] + ins[t + 1] * wb[1:2] + ins[t + 2] * wb[2:3]
          + ins[t + 3] * wb[3:4]) + bias_ref[...] for t in range(nsteps)], axis=0)
    for k in range(CONV_B_W - 1):
        nb_ref[k] = ins[nsteps + k]

    _lru_coeffs(xb, wri_ref, br_ref[...], bi_ref[...], lam_ref[...], a_buf, b_buf)
    gate = jax.nn.gelu(dot(wbg_ref))
    h = h0_ref[...]
    for t in range(nsteps):
        h = a_buf[slab(t), :] * h + b_buf[slab(t), :]
        yb_ref[slab(t), :] = (h * gate[slab(t), :]).astype(BF16)
    hl_ref[...] = h


def _mix_weight_specs(layer, blk):
    w_col = lambda width, off: pl.BlockSpec(
        (None, D_MODEL, width), lambda *g: (0, 0, off // width + blk(*g)))
    vec_b = pl.BlockSpec((None, 1, CB), lambda *g: (layer, 0, blk(*g)))
    return [
        w_col(CA, OFF_A_BG), w_col(CA, OFF_A_CG), w_col(CA, OFF_A_X),
        w_col(CB, OFF_B_X), w_col(CB, OFF_B_GATE),
        w_col(CB, OFF_GATE_A), w_col(CB, OFF_GATE_B),
        pl.BlockSpec((None, CONV_A_W, CA), lambda *g: (layer, 0, blk(*g))),
        pl.BlockSpec((None, CONV_B_W, CB), lambda *g: (layer, 0, blk(*g))),
        vec_b,
        pl.BlockSpec((None, CB // HEAD, HEAD, 2 * HEAD), lambda *g: (layer, blk(*g), 0, 0)),
        vec_b, vec_b, vec_b,
    ]


def _proj_mix_prompt(x, g, w_in16, wa, wb, bias, wri, br, bi, lam, layer, seq,
                     cast_same=None):
    m = x.shape[0]
    tm = TM_MIX
    n_i = m // tm
    n_blocks = n_i * MIX_BLOCKS
    tiles_per_seq = seq // tm
    dot_blk = lambda s: jnp.minimum(s, n_blocks - 1)
    mix_blk = lambda s: jnp.maximum(s - 1, 0)
    dot_c = lambda s: dot_blk(s) % MIX_BLOCKS
    mix_c = lambda s: mix_blk(s) % MIX_BLOCKS
    dot_specs = _mix_weight_specs(layer, dot_c)[:N_W_IN_SLICES]
    mix_specs = _mix_weight_specs(layer, mix_c)[N_W_IN_SLICES:]
    in_specs = [
        pl.BlockSpec((tm, D_MODEL), lambda s: (dot_blk(s) // MIX_BLOCKS, 0)),
        pl.BlockSpec((None, 1, D_MODEL), lambda s: (layer, 0, 0)),
    ] + dot_specs + mix_specs
    args = [x, g] + [w_in16] * N_W_IN_SLICES + [wa, wb, bias, wri, br, bi, lam]
    out_row = lambda width: pl.BlockSpec(
        (tm, width), lambda s: (mix_blk(s) // MIX_BLOCKS, mix_c(s)))
    state = lambda width: pl.BlockSpec(
        (None, SUBLANES, width), lambda s: (mix_blk(s) // MIX_BLOCKS, 0, mix_c(s)))
    out_specs = [out_row(CA), out_row(CB), out_row(CB), out_row(CB),
                 state(CA), state(CB), state(CB)]
    out_shape = [
        jax.ShapeDtypeStruct((m, D_A), BF16),
        jax.ShapeDtypeStruct((m, D_B), BF16),
        jax.ShapeDtypeStruct((m, D_MODEL), BF16),
        jax.ShapeDtypeStruct((m, D_MODEL), BF16),
        jax.ShapeDtypeStruct((n_i, SUBLANES, D_A), F32),
        jax.ShapeDtypeStruct((n_i, SUBLANES, D_B), F32),
        jax.ShapeDtypeStruct((n_i, SUBLANES, D_B), F32),
    ]
    assert (len(in_specs), len(out_specs)) == (PROJ_MIX_INPUTS, PROJ_MIX_OUTPUTS)
    n_cast = 0
    if cast_same is not None:
        chunks = [(w, n_i, MIX_BLOCKS,
                   lambda s: (dot_blk(s) // MIX_BLOCKS, dot_blk(s) % MIX_BLOCKS))
                  for w in cast_same]
        n_cast = len(chunks)
        c_in, c_out, c_shape, c_args = _cast_operands(chunks, layer)
        in_specs, args = in_specs + c_in, args + c_args
        out_specs, out_shape = out_specs + c_out, out_shape + c_shape
    scratch = [
        pltpu.VMEM((tm, D_MODEL), BF16),
        pltpu.VMEM((tm, P_COLS), F32),
        pltpu.VMEM((tm, P_COLS), F32),
        pltpu.VMEM((SUBLANES + tm, CA), F32),
        pltpu.VMEM((SUBLANES + tm, CB), F32),
        pltpu.VMEM((MIX_BLOCKS, SUBLANES, CA), F32),
        pltpu.VMEM((MIX_BLOCKS, SUBLANES, CB), F32),
        pltpu.VMEM((MIX_BLOCKS, SUBLANES, CB), F32),
        pltpu.VMEM((tm, CB), F32),
        pltpu.VMEM((tm, CB), F32),
        pltpu.VMEM((tm, CB), F32),
    ]
    return pl.pallas_call(
        functools.partial(_proj_mix_prompt_kernel, tiles_per_seq, n_blocks, n_cast),
        grid=(n_blocks + 1,),
        in_specs=in_specs,
        out_specs=out_specs,
        out_shape=out_shape,
        scratch_shapes=scratch,
        compiler_params=_params(1),
        name="proj_mix_prompt_cast" if n_cast else "proj_mix_prompt",
    )(*args)


def _proj_mix_sample(x, g, w_in16, wa, wb, bias, wri, br, bi, lam, layer, states):
    m = x.shape[0]
    sa, sb, h0, nb, nsteps = states
    assert m == TM == nb * nsteps
    st_a = lambda k: pl.BlockSpec(
        (None, nb, CA), lambda i, c: (layer, 0, k * MIX_BLOCKS + c))
    st_b = lambda k: pl.BlockSpec(
        (None, nb, CB), lambda i, c: (layer, 0, k * MIX_BLOCKS + c))
    in_specs = [
        pl.BlockSpec((TM, D_MODEL), lambda i, c: (i, 0)),
        pl.BlockSpec((None, 1, D_MODEL), lambda i, c: (layer, 0, 0)),
    ] + _mix_weight_specs(layer, lambda i, c: c) + [
        st_a(0), st_a(1), st_b(0), st_b(1), st_b(2),
        pl.BlockSpec((None, nb, CB), lambda i, c: (layer, 0, c)),
    ]
    args = ([x, g] + [w_in16] * N_W_IN_SLICES
            + [wa, wb, bias, wri, br, bi, lam, sa, sa, sb, sb, sb, h0])
    out_specs = [
        pl.BlockSpec((TM, CA), lambda i, c: (i, c)),
        pl.BlockSpec((TM, CB), lambda i, c: (i, c)),
        pl.BlockSpec((TM, CB), lambda i, c: (i, c)),
        pl.BlockSpec((TM, CB), lambda i, c: (i, c)),
        pl.BlockSpec((CONV_A_W - 1, nb, CA), lambda i, c: (0, 0, c)),
        pl.BlockSpec((CONV_B_W - 1, nb, CB), lambda i, c: (0, 0, c)),
        pl.BlockSpec((nb, CB), lambda i, c: (0, c)),
    ]
    out_shape = [
        jax.ShapeDtypeStruct((m, D_A), BF16),
        jax.ShapeDtypeStruct((m, D_B), BF16),
        jax.ShapeDtypeStruct((m, D_MODEL), BF16),
        jax.ShapeDtypeStruct((m, D_MODEL), BF16),
        jax.ShapeDtypeStruct((CONV_A_W - 1, nb, D_A), F32),
        jax.ShapeDtypeStruct((CONV_B_W - 1, nb, D_B), F32),
        jax.ShapeDtypeStruct((nb, D_B), F32),
    ]
    return pl.pallas_call(
        functools.partial(_proj_mix_sample_kernel, nb, nsteps),
        grid=(m // TM, MIX_BLOCKS),
        in_specs=in_specs,
        out_specs=out_specs,
        out_shape=out_shape,
        scratch_shapes=[pltpu.VMEM((TM, D_MODEL), BF16),
                        pltpu.VMEM((TM, CB), F32), pltpu.VMEM((TM, CB), F32)],
        compiler_params=_params(2),
        name="proj_mix_sample",
    )(*args)


def _merge_kernel(ya_ref, yb_ref, sga_ref, sgb_ref, woa_ref, wob_ref, wo_ref,
                  x_ref, g_ref, gffn_ref, o_ref, v_ref):
    tc = TC_MERGE
    ya, yb = ya_ref[...], yb_ref[...]
    mix = None
    for jb in range(D_MODEL // tc):
        cols = slice(jb * tc, (jb + 1) * tc)
        y_a = jnp.dot(ya, woa_ref[:, cols], preferred_element_type=F32)
        y_b = jnp.dot(yb, wob_ref[:, cols], preferred_element_type=F32)
        merged = (sga_ref[:, cols].astype(F32) * y_a
                  + sgb_ref[:, cols].astype(F32) * y_b).astype(BF16)
        part = jnp.dot(merged, wo_ref[cols, :], preferred_element_type=F32)
        mix = part if mix is None else mix + part
    x_out = x_ref[...] + _rms(mix, g_ref[...])
    o_ref[...] = x_out
    v_ref[...] = _rms(x_out, gffn_ref[...]).astype(BF16)


def _merge(ya, yb, sga, sgb, woa, wob, wo, x, g, g_ffn, layer):
    m = x.shape[0]
    tm = TM_RESIDENT
    rows = lambda width: pl.BlockSpec((tm, width), lambda i: (i, 0))
    resident = lambda k, n: pl.BlockSpec(
        (None, k, n), lambda i: (0, 0, 0), pipeline_mode=pl.Buffered(1))
    return pl.pallas_call(
        _merge_kernel,
        grid=(m // tm,),
        in_specs=[
            rows(D_A), rows(D_B), rows(D_MODEL), rows(D_MODEL),
            resident(D_A, D_MODEL), resident(D_B, D_MODEL), resident(D_MODEL, D_MODEL),
            rows(D_MODEL),
            pl.BlockSpec((None, 1, D_MODEL), lambda i: (layer, 0, 0)),
            pl.BlockSpec((None, 1, D_MODEL), lambda i: (layer, 0, 0)),
        ],
        out_specs=[rows(D_MODEL), rows(D_MODEL)],
        out_shape=[jax.ShapeDtypeStruct((m, D_MODEL), F32),
                   jax.ShapeDtypeStruct((m, D_MODEL), BF16)],
        compiler_params=_params(1),
        name="merge",
    )(ya, yb, sga, sgb, woa, wob, wo, x, g, g_ffn)


def _mlp_up_kernel(n_cast, v_ref, wg_ref, wu_ref, *refs):
    cast_in, h_ref, cast_out = refs[:n_cast], refs[n_cast], refs[n_cast + 1:]
    v = v_ref[...]
    g = jnp.dot(v, wg_ref[...], preferred_element_type=F32)
    _cast_blocks(cast_in, cast_out)
    u = jnp.dot(v, wu_ref[...], preferred_element_type=F32)
    h_ref[...] = ((g * _sigmoid(g)) * u).astype(BF16)


def _mlp_up(v, wgu, layer, cast_next=None):
    m = v.shape[0]
    nh = FFN_HIDDEN // TH_MLP
    n_i = m // TM
    in_specs = [
        pl.BlockSpec((TM, D_MODEL), lambda j, i: (i, 0)),
        pl.BlockSpec((None, D_MODEL, TH_MLP), lambda j, i: (0, 0, j)),
        pl.BlockSpec((None, D_MODEL, TH_MLP), lambda j, i: (0, 0, nh + j)),
    ]
    out_specs = [pl.BlockSpec((TM, TH_MLP), lambda j, i: (i, j))]
    out_shape = [jax.ShapeDtypeStruct((m, FFN_HIDDEN), BF16)]
    args = [v, wgu, wgu]
    n_cast = 0
    if cast_next is not None:
        w_in, w_gu, w_down, w_oa, w_ob, w_o = cast_next
        chunks = [
            (w_in, n_i, nh, lambda j, i: (i, j)),
            (w_gu, n_i, nh, lambda j, i: (i, j)),
            (w_down, nh, n_i, lambda j, i: (j, i)),
            (w_oa, n_i, nh, lambda j, i: (i, j)),
            (w_ob, n_i, nh, lambda j, i: (i, j)),
            (w_o, n_i, nh, lambda j, i: (i, j)),
        ]
        n_cast = len(chunks)
        c_in, c_out, c_shape, c_args = _cast_operands(chunks, layer + 1)
        in_specs, args = in_specs + c_in, args + c_args
        out_specs, out_shape = out_specs + c_out, out_shape + c_shape
    outs = pl.pallas_call(
        functools.partial(_mlp_up_kernel, n_cast),
        grid=(nh, n_i),
        in_specs=in_specs,
        out_specs=out_specs,
        out_shape=out_shape,
        compiler_params=_params(2),
        name="mlp_up_cast" if n_cast else "mlp_up",
    )(*args)
    return outs[0], tuple(outs[1:])


def _mlp_down_kernel(h_ref, wd_ref, x_ref, g_ref, o_ref):
    f = jnp.dot(h_ref[...], wd_ref[...], preferred_element_type=F32)
    o_ref[...] = x_ref[...] + _rms(f, g_ref[...])


def _mlp_down(hidden, wd, x, gpost, layer):
    m = x.shape[0]
    tm = TM_RESIDENT
    rows = lambda width: pl.BlockSpec((tm, width), lambda i: (i, 0))
    return pl.pallas_call(
        _mlp_down_kernel,
        grid=(m // tm,),
        in_specs=[
            rows(FFN_HIDDEN),
            pl.BlockSpec((None, FFN_HIDDEN, D_MODEL), lambda i: (0, 0, 0),
                         pipeline_mode=pl.Buffered(1)),
            rows(D_MODEL),
            pl.BlockSpec((None, 1, D_MODEL), lambda i: (layer, 0, 0)),
        ],
        out_specs=rows(D_MODEL),
        out_shape=jax.ShapeDtypeStruct((m, D_MODEL), F32),
        compiler_params=_params(1),
        name="mlp_down",
    )(hidden, wd, x, gpost)


def kernel(x_prompt, x_sample, state_conv_a, state_conv_b, state_lru_h, w_in, conv_a_w,
           w_out_a, conv_b_w, conv_b_bias, w_r, b_r, w_i, b_i, lru_lambda, w_out_b, w_o,
           norm_pre_mix, norm_post_mix, norm_pre_ffn, norm_post_ffn, w_gate_up, w_down):
    batch, seq, _ = x_prompt.shape
    nb, nsteps, _ = x_sample.shape
    depth = w_in.shape[0]
    assert seq % TM == 0 and seq % TM_MIX == 0 and nb * nsteps == TM
    assert nsteps >= CONV_B_W - 1

    big_w = (w_in, w_gate_up, w_down, w_out_a, w_out_b, w_o)
    w16 = (w_in[:1].astype(BF16),)
    wri16 = jnp.concatenate([w_r, w_i], axis=-1).astype(BF16)
    row = lambda v: v.reshape(depth, 1, v.shape[-1])
    bias, br, bi, lam = row(conv_b_bias), row(b_r), row(b_i), row(lru_lambda)
    g_pre_mix, g_post_mix = row(norm_pre_mix), row(norm_post_mix)
    g_pre_ffn, g_post_ffn = row(norm_pre_ffn), row(norm_post_ffn)
    sa = state_conv_a.reshape(depth, nb, (CONV_A_W - 1) * D_A)
    sb = state_conv_b.reshape(depth, nb, (CONV_B_W - 1) * D_B)

    xp = x_prompt.reshape(batch * seq, D_MODEL)
    xs = x_sample.transpose(1, 0, 2).reshape(nsteps * nb, D_MODEL)

    pa, pb, ph, s_a, s_b, s_h = [], [], [], [], [], []
    for l in range(depth):
        mix_w = (w16[0], conv_a_w, conv_b_w, bias, wri16, br, bi, lam, l)
        ya_p, yb_p, sga_p, sgb_p, na_p, nb_p, hl_p, *cast = _proj_mix_prompt(
            xp, g_pre_mix, *mix_w, seq, cast_same=big_w[1:] if l == 0 else None)
        w16 = w16 + tuple(cast)
        w_in16, w_gu16, w_down16, w_out_a16, w_out_b16, w_o16 = w16
        ya_s, yb_s, sga_s, sgb_s, na_s, nb_s, hl_s = _proj_mix_sample(
            xs, g_pre_mix, *mix_w, (sa, sb, state_lru_h, nb, nsteps))
        out_w = (w_out_a16, w_out_b16, w_o16)
        xp, vp = _merge(ya_p, yb_p, sga_p, sgb_p, *out_w, xp, g_post_mix, g_pre_ffn, l)
        xs, vs = _merge(ya_s, yb_s, sga_s, sgb_s, *out_w, xs, g_post_mix, g_pre_ffn, l)
        hp, w16 = _mlp_up(vp, w_gu16, l, cast_next=big_w if l + 1 < depth else None)
        hs, _ = _mlp_up(vs, w_gu16, l)
        xp = _mlp_down(hp, w_down16, xp, g_post_ffn, l)
        xs = _mlp_down(hs, w_down16, xs, g_post_ffn, l)
        last = slice(seq // TM_MIX - 1, None, seq // TM_MIX)
        pa.append(na_p[last, SUBLANES - (CONV_A_W - 1):, :])
        pb.append(nb_p[last, SUBLANES - (CONV_B_W - 1):, :])
        ph.append(hl_p[last, 0, :])
        s_a.append(na_s)
        s_b.append(nb_s)
        s_h.append(hl_s)

    y_prompt = xp.reshape(batch, seq, D_MODEL)
    y_sample = xs.reshape(nsteps, nb, D_MODEL).transpose(1, 0, 2)
    return (y_prompt, y_sample, jnp.stack(pa), jnp.stack(pb), jnp.stack(ph),
            jnp.stack(s_a).transpose(0, 2, 1, 3), jnp.stack(s_b).transpose(0, 2, 1, 3),
            jnp.stack(s_h))
```

```python
import functools

import jax
import jax.numpy as jnp
from jax import lax
from jax.experimental import pallas as pl
from jax.experimental.pallas import tpu as pltpu

F32 = jnp.float32
BF16 = jnp.bfloat16

D_MODEL = 2048
D_A = D_MODEL // 2
D_B = D_MODEL
CONV_A_W = 3
CONV_B_W = 4
HEAD = 128
C_LRU = 8.0
LOG2_E = 1.4426950408889634
FFN_HIDDEN = 5632
EPS = 1e-6
IN_COLS = 3 * D_A + 2 * D_B + 2 * D_MODEL

OFF_A_BG, OFF_A_CG, OFF_A_X = 0, D_A, 2 * D_A
OFF_B_X = 3 * D_A
OFF_B_GATE = OFF_B_X + D_B
OFF_GATE_A = OFF_B_GATE + D_B
OFF_GATE_B = OFF_GATE_A + D_MODEL

SUBLANES = 8
BF16_ROWS = 16
LANES = 128
TM = 512
TC_MERGE = 512
TM_RESIDENT = 256
TH_MLP = 1408
CAST_TILES = 16
MIX_BLOCKS = 4
CA = D_A // MIX_BLOCKS
CB = D_B // MIX_BLOCKS
VMEM_LIMIT = 56 * 1024 * 1024


def _params(n_axes):
    return pltpu.CompilerParams(
        dimension_semantics=("arbitrary",) * n_axes,
        vmem_limit_bytes=VMEM_LIMIT)


def _rms(x, g):
    var = jnp.mean(x * x, axis=-1, keepdims=True)
    return x * lax.rsqrt(var + EPS) * g


def _sigmoid(x):
    return 0.5 * jnp.tanh(0.5 * x) + 0.5


def _log_sigmoid(x):
    return jnp.minimum(x, 0.0) - jnp.log1p(jnp.exp(-jnp.abs(x)))


def _head(h):
    return slice(h * HEAD, (h + 1) * HEAD)


def _gate_pre(xb, wri_ref, h):
    return jnp.dot(xb[:, _head(h)].astype(BF16), wri_ref[h], preferred_element_type=F32)


def _lru_coeffs_head(xb, pre, b_r, b_i, lam, a_ref, b_ref, h):
    sl = _head(h)
    r = _sigmoid(pre[:, :HEAD] + b_r[:, sl])
    i = _sigmoid(pre[:, HEAD:] + b_i[:, sl])
    a = jnp.exp2(r * ((C_LRU * LOG2_E) * _log_sigmoid(lam[:, sl])))
    y = 1.0 - a * a
    mult = jnp.where(y > 0.0, y * lax.rsqrt(y), 0.0)
    a_ref[:, sl] = a
    b_ref[:, sl] = mult * i * xb[:, sl]


def _lru_coeffs(xb, wri_ref, b_r, b_i, lam, a_ref, b_ref):
    for h in range(CB // HEAD):
        _lru_coeffs_head(xb, _gate_pre(xb, wri_ref, h), b_r, b_i, lam, a_ref, b_ref, h)


def _scan_groups(a_ref, b_ref, h_ref, h_in, n_groups):
    row = lax.broadcasted_iota(jnp.int32, (SUBLANES, CB), 0)

    h_prev = h_in
    for g in range(n_groups):
        rows = slice(g * SUBLANES, (g + 1) * SUBLANES)
        a = a_ref[rows, :]
        b = b_ref[rows, :]
        for s in (1, 2, 4):
            keep = row >= s
            a_sh = jnp.where(keep, pltpu.roll(a, s, 0), 1.0)
            b_sh = jnp.where(keep, pltpu.roll(b, s, 0), 0.0)
            b = a * b_sh + b
            a = a * a_sh
        h = a * h_prev + b
        h_ref[rows, :] = h
        h_prev = jnp.broadcast_to(h[SUBLANES - 1:SUBLANES, :], (SUBLANES, CB))
    return h_prev


def _norm(x_ref, g_ref, u_ref):
    @pl.when(pl.program_id(1) == 0)
    def _():
        u_ref[...] = _rms(x_ref[...], g_ref[...]).astype(BF16)

    return u_ref[...]


def _gates(u, wga_ref, wgb_ref, sga_ref, sgb_ref):
    sga_ref[...] = _sigmoid(
        jnp.dot(u, wga_ref[...], preferred_element_type=F32)).astype(BF16)
    sgb_ref[...] = _sigmoid(
        jnp.dot(u, wgb_ref[...], preferred_element_type=F32)).astype(BF16)


def _cast_operands(chunks, src_layer):
    in_specs, out_specs, out_shape, args = [], [], [], []
    for w, row_blocks, col_blocks, idx in chunks:
        rows, cols = w.shape[1] // row_blocks, w.shape[2] // col_blocks
        assert (rows * row_blocks, cols * col_blocks) == w.shape[1:], w.shape
        assert rows % BF16_ROWS == 0 and cols % LANES == 0, (w.shape, rows, cols)
        in_specs.append(pl.BlockSpec(
            (None, rows, cols), lambda *g, idx=idx: (src_layer,) + idx(*g)))
        out_specs.append(pl.BlockSpec(
            (None, rows, cols), lambda *g, idx=idx: (0,) + idx(*g)))
        out_shape.append(jax.ShapeDtypeStruct((1,) + w.shape[1:], BF16))
        args.append(w)
    return in_specs, out_specs, out_shape, args


def _cast_blocks(cast_in, cast_out):
    for src, dst in zip(cast_in, cast_out):
        dst[...] = src[...].astype(BF16)


PROJ_MIX_INPUTS = 16
PROJ_MIX_OUTPUTS = 7
N_W_IN_SLICES = 7


def _proj_mix_prompt_kernel(tiles_per_seq, n_cast, *refs):
    (x_ref, g_ref, wabg_ref, wacg_ref, wax_ref, wbx_ref, wbg_ref, wga_ref, wgb_ref,
     wa_ref, wb_ref, bias_ref, wri_ref, br_ref, bi_ref, lam_ref) = refs[:PROJ_MIX_INPUTS]
    refs = refs[PROJ_MIX_INPUTS:]
    cast_in, refs = refs[:n_cast], refs[n_cast:]
    ya_ref, yb_ref, sga_ref, sgb_ref, na_ref, nb_ref, hl_ref = refs[:PROJ_MIX_OUTPUTS]
    refs = refs[PROJ_MIX_OUTPUTS:]
    cast_out, refs = refs[:n_cast], refs[n_cast:]
    (u_ref, exta_ref, extb_ref, taila_ref, tailb_ref, hc_ref, a_buf, b_buf, h_buf) = refs
    c = pl.program_id(1)
    u = _norm(x_ref, g_ref, u_ref)
    dot = lambda w_ref: jnp.dot(u, w_ref[...], preferred_element_type=F32)

    @pl.when(pl.program_id(0) % tiles_per_seq == 0)
    def _():
        taila_ref[c] = jnp.zeros((SUBLANES, CA), F32)
        tailb_ref[c] = jnp.zeros((SUBLANES, CB), F32)
        hc_ref[c] = jnp.zeros((SUBLANES, CB), F32)

    pre_ga = dot(wga_ref)
    _cast_blocks(cast_in, cast_out)
    bx = dot(wbx_ref)
    sga_ref[...] = _sigmoid(pre_ga).astype(BF16)

    pre_gb = dot(wgb_ref)
    extb_ref[0:SUBLANES, :] = tailb_ref[c]
    extb_ref[SUBLANES:, :] = bx
    wb = wb_ref[...]
    xb = (extb_ref[SUBLANES - 3:SUBLANES - 3 + TM, :] * wb[0:1]
          + extb_ref[SUBLANES - 2:SUBLANES - 2 + TM, :] * wb[1:2]
          + extb_ref[SUBLANES - 1:SUBLANES - 1 + TM, :] * wb[2:3]
          + bx * wb[3:4]) + bias_ref[...]
    tail_b = extb_ref[TM:TM + SUBLANES, :]
    nb_ref[...] = tail_b
    tailb_ref[c] = tail_b

    heads = range(CB // HEAD)
    pres = [_gate_pre(xb, wri_ref, h) for h in heads]
    acg = dot(wacg_ref)
    sgb_ref[...] = _sigmoid(pre_gb).astype(BF16)
    coeffs = functools.partial(_lru_coeffs_head, xb, b_r=br_ref[...], b_i=bi_ref[...],
                               lam=lam_ref[...], a_ref=a_buf, b_ref=b_buf)
    ax = dot(wax_ref)
    coeffs(pre=pres[0], h=0)
    bg = dot(wbg_ref)
    coeffs(pre=pres[1], h=1)
    coeffs(pre=pres[2], h=2)

    z = acg * ax
    exta_ref[0:SUBLANES, :] = taila_ref[c]
    exta_ref[SUBLANES:, :] = z
    wa = wa_ref[...]
    conv = (exta_ref[SUBLANES - 2:SUBLANES - 2 + TM, :] * wa[0:1]
            + exta_ref[SUBLANES - 1:SUBLANES - 1 + TM, :] * wa[1:2]
            + z * wa[2:3])
    tail_a = exta_ref[TM:TM + SUBLANES, :]
    na_ref[...] = tail_a
    taila_ref[c] = tail_a

    coeffs(pre=pres[3], h=3)
    abg = dot(wabg_ref)
    h_out = _scan_groups(a_buf, b_buf, h_buf, hc_ref[c], TM // SUBLANES)
    hc_ref[c] = h_out
    hl_ref[...] = h_out
    yb_ref[...] = (h_buf[...] * jax.nn.gelu(bg)).astype(BF16)
    ya_ref[...] = (abg * conv).astype(BF16)


def _proj_mix_sample_kernel(nb, nsteps, x_ref, g_ref,
                            wabg_ref, wacg_ref, wax_ref, wbx_ref, wbg_ref, wga_ref, wgb_ref,
                            wa_ref, wb_ref, bias_ref, wri_ref, br_ref, bi_ref, lam_ref,
                            sa0_ref, sa1_ref, sb0_ref, sb1_ref, sb2_ref, h0_ref,
                            ya_ref, yb_ref, sga_ref, sgb_ref, na_ref, nb_ref, hl_ref,
                            u_ref, a_buf, b_buf):
    u = _norm(x_ref, g_ref, u_ref)
    _gates(u, wga_ref, wgb_ref, sga_ref, sgb_ref)
    dot = lambda w_ref: jnp.dot(u, w_ref[...], preferred_element_type=F32)
    slab = lambda t: slice(t * nb, (t + 1) * nb)

    z = dot(wacg_ref) * dot(wax_ref)
    abg = dot(wabg_ref)
    ins = [sa0_ref[...], sa1_ref[...]] + [z[slab(t), :] for t in range(nsteps)]
    wa = wa_ref[...]
    for t in range(nsteps):
        conv = ins[t] * wa[0:1] + ins[t + 1] * wa[1:2] + ins[t + 2] * wa[2:3]
        ya_ref[slab(t), :] = (abg[slab(t), :] * conv).astype(BF16)
    for k in range(CONV_A_W - 1):
        na_ref[k] = ins[nsteps + k]

    bx = dot(wbx_ref)
    ins = ([sb0_ref[...], sb1_ref[...], sb2_ref[...]]
           + [bx[slab(t), :] for t in range(nsteps)])
    wb = wb_ref[...]
    xb = jnp.concatenate(
        [(ins[t] * wb[0:1] + ins[t + 1] * wb[1:2] + ins[t + 2] * wb[2:3]
          + ins[t + 3] * wb[3:4]) + bias_ref[...] for t in range(nsteps)], axis=0)
    for k in range(CONV_B_W - 1):
        nb_ref[k] = ins[nsteps + k]

    _lru_coeffs(xb, wri_ref, br_ref[...], bi_ref[...], lam_ref[...], a_buf, b_buf)
    gate = jax.nn.gelu(dot(wbg_ref))
    h = h0_ref[...]
    for t in range(nsteps):
        h = a_buf[slab(t), :] * h + b_buf[slab(t), :]
        yb_ref[slab(t), :] = (h * gate[slab(t), :]).astype(BF16)
    hl_ref[...] = h


def _mix_weight_specs(layer, blk):
    w_col = lambda width, off: pl.BlockSpec(
        (None, D_MODEL, width), lambda *g: (0, 0, off // width + blk(*g)))
    vec_b = pl.BlockSpec((None, 1, CB), lambda *g: (layer, 0, blk(*g)))
    return [
        w_col(CA, OFF_A_BG), w_col(CA, OFF_A_CG), w_col(CA, OFF_A_X),
        w_col(CB, OFF_B_X), w_col(CB, OFF_B_GATE),
        w_col(CB, OFF_GATE_A), w_col(CB, OFF_GATE_B),
        pl.BlockSpec((None, CONV_A_W, CA), lambda *g: (layer, 0, blk(*g))),
        pl.BlockSpec((None, CONV_B_W, CB), lambda *g: (layer, 0, blk(*g))),
        vec_b,
        pl.BlockSpec((None, CB // HEAD, HEAD, 2 * HEAD), lambda *g: (layer, blk(*g), 0, 0)),
        vec_b, vec_b, vec_b,
    ]


def _proj_mix_prompt(x, g, w_in16, wa, wb, bias, wri, br, bi, lam, layer, seq, n_tiles,
                     cast_same=None):
    m = n_tiles * TM
    tiles_per_seq = seq // TM
    in_specs = [
        pl.BlockSpec((TM, D_MODEL), lambda i, c: (i, 0)),
        pl.BlockSpec((None, 1, D_MODEL), lambda i, c: (layer, 0, 0)),
    ] + _mix_weight_specs(layer, lambda i, c: c)
    args = [x, g] + [w_in16] * N_W_IN_SLICES + [wa, wb, bias, wri, br, bi, lam]
    out_row = lambda width: pl.BlockSpec((TM, width), lambda i, c: (i, c))
    state = lambda width: pl.BlockSpec((None, SUBLANES, width), lambda i, c: (i, 0, c))
    out_specs = [out_row(CA), out_row(CB), out_row(CB), out_row(CB),
                 state(CA), state(CB), state(CB)]
    out_shape = [
        jax.ShapeDtypeStruct((m, D_A), BF16),
        jax.ShapeDtypeStruct((m, D_B), BF16),
        jax.ShapeDtypeStruct((m, D_MODEL), BF16),
        jax.ShapeDtypeStruct((m, D_MODEL), BF16),
        jax.ShapeDtypeStruct((n_tiles, SUBLANES, D_A), F32),
        jax.ShapeDtypeStruct((n_tiles, SUBLANES, D_B), F32),
        jax.ShapeDtypeStruct((n_tiles, SUBLANES, D_B), F32),
    ]
    assert (len(in_specs), len(out_specs)) == (PROJ_MIX_INPUTS, PROJ_MIX_OUTPUTS)
    n_cast = 0
    if cast_same is not None:
        chunks = [(w, n_tiles, MIX_BLOCKS, lambda i, c: (i, c)) for w in cast_same]
        n_cast = len(chunks)
        c_in, c_out, c_shape, c_args = _cast_operands(chunks, layer)
        in_specs, args = in_specs + c_in, args + c_args
        out_specs, out_shape = out_specs + c_out, out_shape + c_shape
    scratch = [
        pltpu.VMEM((TM, D_MODEL), BF16),
        pltpu.VMEM((SUBLANES + TM, CA), F32),
        pltpu.VMEM((SUBLANES + TM, CB), F32),
        pltpu.VMEM((MIX_BLOCKS, SUBLANES, CA), F32),
        pltpu.VMEM((MIX_BLOCKS, SUBLANES, CB), F32),
        pltpu.VMEM((MIX_BLOCKS, SUBLANES, CB), F32),
        pltpu.VMEM((TM, CB), F32),
        pltpu.VMEM((TM, CB), F32),
        pltpu.VMEM((TM, CB), F32),
    ]
    return pl.pallas_call(
        functools.partial(_proj_mix_prompt_kernel, tiles_per_seq, n_cast),
        grid=(n_tiles, MIX_BLOCKS),
        in_specs=in_specs,
        out_specs=out_specs,
        out_shape=out_shape,
        scratch_shapes=scratch,
        compiler_params=_params(2),
        name="proj_mix_prompt_cast" if n_cast else "proj_mix_prompt",
    )(*args)


def _proj_mix_sample(x, g, w_in16, wa, wb, bias, wri, br, bi, lam, layer, states,
                     row_tile):
    sa, sb, h0, nb, nsteps = states
    m = nb * nsteps
    assert m == TM
    st_a = lambda k: pl.BlockSpec(
        (None, nb, CA), lambda i, c: (layer, 0, k * MIX_BLOCKS + c))
    st_b = lambda k: pl.BlockSpec(
        (None, nb, CB), lambda i, c: (layer, 0, k * MIX_BLOCKS + c))
    in_specs = [
        pl.BlockSpec((TM, D_MODEL), lambda i, c: (row_tile + i, 0)),
        pl.BlockSpec((None, 1, D_MODEL), lambda i, c: (layer, 0, 0)),
    ] + _mix_weight_specs(layer, lambda i, c: c) + [
        st_a(0), st_a(1), st_b(0), st_b(1), st_b(2),
        pl.BlockSpec((None, nb, CB), lambda i, c: (layer, 0, c)),
    ]
    args = ([x, g] + [w_in16] * N_W_IN_SLICES
            + [wa, wb, bias, wri, br, bi, lam, sa, sa, sb, sb, sb, h0])
    out_specs = [
        pl.BlockSpec((TM, CA), lambda i, c: (i, c)),
        pl.BlockSpec((TM, CB), lambda i, c: (i, c)),
        pl.BlockSpec((TM, CB), lambda i, c: (i, c)),
        pl.BlockSpec((TM, CB), lambda i, c: (i, c)),
        pl.BlockSpec((CONV_A_W - 1, nb, CA), lambda i, c: (0, 0, c)),
        pl.BlockSpec((CONV_B_W - 1, nb, CB), lambda i, c: (0, 0, c)),
        pl.BlockSpec((nb, CB), lambda i, c: (0, c)),
    ]
    out_shape = [
        jax.ShapeDtypeStruct((m, D_A), BF16),
        jax.ShapeDtypeStruct((m, D_B), BF16),
        jax.ShapeDtypeStruct((m, D_MODEL), BF16),
        jax.ShapeDtypeStruct((m, D_MODEL), BF16),
        jax.ShapeDtypeStruct((CONV_A_W - 1, nb, D_A), F32),
        jax.ShapeDtypeStruct((CONV_B_W - 1, nb, D_B), F32),
        jax.ShapeDtypeStruct((nb, D_B), F32),
    ]
    return pl.pallas_call(
        functools.partial(_proj_mix_sample_kernel, nb, nsteps),
        grid=(m // TM, MIX_BLOCKS),
        in_specs=in_specs,
        out_specs=out_specs,
        out_shape=out_shape,
        scratch_shapes=[pltpu.VMEM((TM, D_MODEL), BF16),
                        pltpu.VMEM((TM, CB), F32), pltpu.VMEM((TM, CB), F32)],
        compiler_params=_params(2),
        name="proj_mix_sample",
    )(*args)


def _merge_kernel(n_prompt, split_x, *refs):
    (ya_p, ya_s, yb_p, yb_s, sga_p, sga_s, sgb_p, sgb_s, woa_ref, wob_ref, wo_ref) = refs[:11]
    x_refs = refs[11:13] if split_x else refs[11:12]
    g_ref, gffn_ref, o_ref, v_ref = refs[11 + len(x_refs):]
    is_prompt = pl.program_id(0) < n_prompt
    pick = lambda p_ref, s_ref: jnp.where(is_prompt, p_ref[...], s_ref[...])
    ya, yb = pick(ya_p, ya_s), pick(yb_p, yb_s)
    sga, sgb = pick(sga_p, sga_s), pick(sgb_p, sgb_s)
    x = pick(*x_refs) if split_x else x_refs[0][...]
    tc = TC_MERGE
    mix = None
    for jb in range(D_MODEL // tc):
        cols = slice(jb * tc, (jb + 1) * tc)
        y_a = jnp.dot(ya, woa_ref[:, cols], preferred_element_type=F32)
        y_b = jnp.dot(yb, wob_ref[:, cols], preferred_element_type=F32)
        merged = (sga[:, cols].astype(F32) * y_a
                  + sgb[:, cols].astype(F32) * y_b).astype(BF16)
        part = jnp.dot(merged, wo_ref[cols, :], preferred_element_type=F32)
        mix = part if mix is None else mix + part
    x_out = x + _rms(mix, g_ref[...])
    o_ref[...] = x_out
    v_ref[...] = _rms(x_out, gffn_ref[...]).astype(BF16)


def _merge(ya, yb, sga, sgb, woa, wob, wo, x, g, g_ffn, layer):
    tm = TM_RESIDENT
    n_prompt, n_sample = ya[0].shape[0] // tm, ya[1].shape[0] // tm
    split_x = isinstance(x, tuple)
    prompt = lambda width: pl.BlockSpec(
        (tm, width), lambda i: (jnp.minimum(i, n_prompt - 1), 0))
    sample = lambda width: pl.BlockSpec(
        (tm, width), lambda i: (jnp.maximum(i - n_prompt, 0), 0))
    rows = lambda width: pl.BlockSpec((tm, width), lambda i: (i, 0))
    resident = lambda k, n: pl.BlockSpec(
        (None, k, n), lambda i: (0, 0, 0), pipeline_mode=pl.Buffered(1))
    vec = pl.BlockSpec((None, 1, D_MODEL), lambda i: (layer, 0, 0))
    x_specs = [prompt(D_MODEL), sample(D_MODEL)] if split_x else [rows(D_MODEL)]
    m = (n_prompt + n_sample) * tm
    return pl.pallas_call(
        functools.partial(_merge_kernel, n_prompt, split_x),
        grid=(n_prompt + n_sample,),
        in_specs=[
            prompt(D_A), sample(D_A), prompt(D_B), sample(D_B),
            prompt(D_MODEL), sample(D_MODEL), prompt(D_MODEL), sample(D_MODEL),
            resident(D_A, D_MODEL), resident(D_B, D_MODEL), resident(D_MODEL, D_MODEL),
        ] + x_specs + [vec, vec],
        out_specs=[rows(D_MODEL), rows(D_MODEL)],
        out_shape=[jax.ShapeDtypeStruct((m, D_MODEL), F32),
                   jax.ShapeDtypeStruct((m, D_MODEL), BF16)],
        compiler_params=_params(1),
        name="merge",
    )(*ya, *yb, *sga, *sgb, woa, wob, wo, *(x if split_x else (x,)), g, g_ffn)


def _mlp_up_kernel(n_cast, v_ref, wg_ref, wu_ref, *refs):
    cast_in, h_ref, cast_out = refs[:n_cast], refs[n_cast], refs[n_cast + 1:]
    v = v_ref[...]
    g = jnp.dot(v, wg_ref[...], preferred_element_type=F32)
    _cast_blocks(cast_in, cast_out)
    u = jnp.dot(v, wu_ref[...], preferred_element_type=F32)
    h_ref[...] = ((g * _sigmoid(g)) * u).astype(BF16)


def _mlp_up(v, wgu, layer, cast_next=None):
    m = v.shape[0]
    nh = FFN_HIDDEN // TH_MLP
    n_i = m // TM
    in_specs = [
        pl.BlockSpec((TM, D_MODEL), lambda j, i: (i, 0)),
        pl.BlockSpec((None, D_MODEL, TH_MLP), lambda j, i: (0, 0, j)),
        pl.BlockSpec((None, D_MODEL, TH_MLP), lambda j, i: (0, 0, nh + j)),
    ]
    out_specs = [pl.BlockSpec((TM, TH_MLP), lambda j, i: (i, j))]
    out_shape = [jax.ShapeDtypeStruct((m, FFN_HIDDEN), BF16)]
    args = [v, wgu, wgu]
    n_cast = 0
    if cast_next is not None:
        w_in, w_gu, w_down, w_oa, w_ob, w_o = cast_next
        nc = CAST_TILES
        assert nc <= n_i
        ci = lambda i: jnp.minimum(i, nc - 1)
        chunks = [
            (w_in, nc, nh, lambda j, i: (ci(i), j)),
            (w_gu, nc, nh, lambda j, i: (ci(i), j)),
            (w_down, nh, nc, lambda j, i: (j, ci(i))),
            (w_oa, nc, nh, lambda j, i: (ci(i), j)),
            (w_ob, nc, nh, lambda j, i: (ci(i), j)),
            (w_o, nc, nh, lambda j, i: (ci(i), j)),
        ]
        n_cast = len(chunks)
        c_in, c_out, c_shape, c_args = _cast_operands(chunks, layer + 1)
        in_specs, args = in_specs + c_in, args + c_args
        out_specs, out_shape = out_specs + c_out, out_shape + c_shape
    outs = pl.pallas_call(
        functools.partial(_mlp_up_kernel, n_cast),
        grid=(nh, n_i),
        in_specs=in_specs,
        out_specs=out_specs,
        out_shape=out_shape,
        compiler_params=_params(2),
        name="mlp_up_cast" if n_cast else "mlp_up",
    )(*args)
    return outs[0], tuple(outs[1:])


def _mlp_down_kernel(n_prompt, h_ref, wd_ref, x_ref, g_ref, *o_refs):
    f = jnp.dot(h_ref[...], wd_ref[...], preferred_element_type=F32)
    out = x_ref[...] + _rms(f, g_ref[...])
    if n_prompt is None:
        o_refs[0][...] = out
    else:
        @pl.when(pl.program_id(0) < n_prompt)
        def _():
            o_refs[0][...] = out

        @pl.when(pl.program_id(0) >= n_prompt)
        def _():
            o_refs[1][...] = out


def _mlp_down(hidden, wd, x, gpost, layer, split_rows=None):
    m = x.shape[0]
    tm = TM_RESIDENT
    rows = lambda width: pl.BlockSpec((tm, width), lambda i: (i, 0))
    if split_rows is None:
        n_prompt = None
        out_specs = rows(D_MODEL)
        out_shape = jax.ShapeDtypeStruct((m, D_MODEL), F32)
    else:
        n_prompt = split_rows // tm
        out_specs = [
            pl.BlockSpec((tm, D_MODEL), lambda i: (jnp.minimum(i, n_prompt - 1), 0)),
            pl.BlockSpec((tm, D_MODEL), lambda i: (jnp.maximum(i - n_prompt, 0), 0)),
        ]
        out_shape = [jax.ShapeDtypeStruct((split_rows, D_MODEL), F32),
                     jax.ShapeDtypeStruct((m - split_rows, D_MODEL), F32)]
    return pl.pallas_call(
        functools.partial(_mlp_down_kernel, n_prompt),
        grid=(m // tm,),
        in_specs=[
            rows(FFN_HIDDEN),
            pl.BlockSpec((None, FFN_HIDDEN, D_MODEL), lambda i: (0, 0, 0),
                         pipeline_mode=pl.Buffered(1)),
            rows(D_MODEL),
            pl.BlockSpec((None, 1, D_MODEL), lambda i: (layer, 0, 0)),
        ],
        out_specs=out_specs,
        out_shape=out_shape,
        compiler_params=_params(1),
        name="mlp_down",
    )(hidden, wd, x, gpost)


def kernel(x_prompt, x_sample, state_conv_a, state_conv_b, state_lru_h, w_in, conv_a_w,
           w_out_a, conv_b_w, conv_b_bias, w_r, b_r, w_i, b_i, lru_lambda, w_out_b, w_o,
           norm_pre_mix, norm_post_mix, norm_pre_ffn, norm_post_ffn, w_gate_up, w_down):
    batch, seq, _ = x_prompt.shape
    nb, nsteps, _ = x_sample.shape
    depth = w_in.shape[0]
    assert seq % TM == 0 and nb * nsteps == TM
    assert nsteps >= CONV_B_W - 1

    big_w = (w_in, w_gate_up, w_down, w_out_a, w_out_b, w_o)
    w16 = (w_in[:1].astype(BF16),)
    wri16 = jnp.concatenate([w_r, w_i], axis=-1).astype(BF16)
    row = lambda v: v.reshape(depth, 1, v.shape[-1])
    bias, br, bi, lam = row(conv_b_bias), row(b_r), row(b_i), row(lru_lambda)
    g_pre_mix, g_post_mix = row(norm_pre_mix), row(norm_post_mix)
    g_pre_ffn, g_post_ffn = row(norm_pre_ffn), row(norm_post_ffn)
    sa = state_conv_a.reshape(depth, nb, (CONV_A_W - 1) * D_A)
    sb = state_conv_b.reshape(depth, nb, (CONV_B_W - 1) * D_B)

    xp = x_prompt.reshape(batch * seq, D_MODEL)
    xs = x_sample.transpose(1, 0, 2).reshape(nsteps * nb, D_MODEL)

    n_prompt_rows = batch * seq
    x_all = None
    pa, pb, ph, s_a, s_b, s_h = [], [], [], [], [], []
    for l in range(depth):
        mix_w = (w16[0], conv_a_w, conv_b_w, bias, wri16, br, bi, lam, l)
        ya_p, yb_p, sga_p, sgb_p, na_p, nb_p, hl_p, *cast = _proj_mix_prompt(
            xp if l == 0 else x_all, g_pre_mix, *mix_w, seq, n_prompt_rows // TM,
            cast_same=big_w[1:] if l == 0 else None)
        w16 = w16 + tuple(cast)
        w_in16, w_gu16, w_down16, w_out_a16, w_out_b16, w_o16 = w16
        ya_s, yb_s, sga_s, sgb_s, na_s, nb_s, hl_s = _proj_mix_sample(
            xs if l == 0 else x_all, g_pre_mix, *mix_w,
            (sa, sb, state_lru_h, nb, nsteps), 0 if l == 0 else n_prompt_rows // TM)
        x_all, v_all = _merge(
            (ya_p, ya_s), (yb_p, yb_s), (sga_p, sga_s), (sgb_p, sgb_s),
            w_out_a16, w_out_b16, w_o16, (xp, xs) if l == 0 else x_all,
            g_post_mix, g_pre_ffn, l)
        h_all, w16 = _mlp_up(v_all, w_gu16, l, cast_next=big_w if l + 1 < depth else None)
        if l + 1 < depth:
            x_all = _mlp_down(h_all, w_down16, x_all, g_post_ffn, l)
        else:
            xp, xs = _mlp_down(h_all, w_down16, x_all, g_post_ffn, l,
                               split_rows=n_prompt_rows)
        last = slice(seq // TM - 1, None, seq // TM)
        pa.append(na_p[last, SUBLANES - (CONV_A_W - 1):, :])
        pb.append(nb_p[last, SUBLANES - (CONV_B_W - 1):, :])
        ph.append(hl_p[last, 0, :])
        s_a.append(na_s)
        s_b.append(nb_s)
        s_h.append(hl_s)

    y_prompt = xp.reshape(batch, seq, D_MODEL)
    y_sample = xs.reshape(nsteps, nb, D_MODEL).transpose(1, 0, 2)
    return (y_prompt, y_sample, jnp.stack(pa), jnp.stack(pb), jnp.stack(ph),
            jnp.stack(s_a).transpose(0, 2, 1, 3), jnp.stack(s_b).transpose(0, 2, 1, 3),
            jnp.stack(s_h))
```

```python
import functools

import jax
import jax.numpy as jnp
from jax import lax
from jax.experimental import pallas as pl
from jax.experimental.pallas import tpu as pltpu

F32 = jnp.float32
BF16 = jnp.bfloat16

D_MODEL = 2048
D_A = D_MODEL // 2
D_B = D_MODEL
CONV_A_W = 3
CONV_B_W = 4
HEAD = 128
C_LRU = 8.0
LOG2_E = 1.4426950408889634
FFN_HIDDEN = 5632
EPS = 1e-6
IN_COLS = 3 * D_A + 2 * D_B + 2 * D_MODEL

OFF_A_BG, OFF_A_CG, OFF_A_X = 0, D_A, 2 * D_A
OFF_B_X = 3 * D_A
OFF_B_GATE = OFF_B_X + D_B
OFF_GATE_A = OFF_B_GATE + D_B
OFF_GATE_B = OFF_GATE_A + D_MODEL

SUBLANES = 8
BF16_ROWS = 16
LANES = 128
TM = 512
TC_MERGE = 512
TM_RESIDENT = 256
TH_MLP = 1408
CAST_TILES = 16
GU_ROW_BLOCKS = 8
MIX_BLOCKS = 4
CA = D_A // MIX_BLOCKS
CB = D_B // MIX_BLOCKS
VMEM_LIMIT = 56 * 1024 * 1024


def _params(n_axes):
    return pltpu.CompilerParams(
        dimension_semantics=("arbitrary",) * n_axes,
        vmem_limit_bytes=VMEM_LIMIT)


def _rms(x, g):
    var = jnp.mean(x * x, axis=-1, keepdims=True)
    return x * lax.rsqrt(var + EPS) * g


def _sigmoid(x):
    return 0.5 * jnp.tanh(0.5 * x) + 0.5


def _log_sigmoid(x):
    return jnp.minimum(x, 0.0) - jnp.log1p(jnp.exp(-jnp.abs(x)))


def _head(h):
    return slice(h * HEAD, (h + 1) * HEAD)


def _gate_pre(xb, wri_ref, h):
    return jnp.dot(xb[:, _head(h)].astype(BF16), wri_ref[h], preferred_element_type=F32)


def _lru_coeffs_head(xb, pre, b_r, b_i, lam, a_ref, b_ref, h):
    sl = _head(h)
    r = _sigmoid(pre[:, :HEAD] + b_r[:, sl])
    i = _sigmoid(pre[:, HEAD:] + b_i[:, sl])
    a = jnp.exp2(r * ((C_LRU * LOG2_E) * _log_sigmoid(lam[:, sl])))
    y = 1.0 - a * a
    mult = jnp.where(y > 0.0, y * lax.rsqrt(y), 0.0)
    a_ref[:, sl] = a
    b_ref[:, sl] = mult * i * xb[:, sl]


def _lru_coeffs(xb, wri_ref, b_r, b_i, lam, a_ref, b_ref):
    for h in range(CB // HEAD):
        _lru_coeffs_head(xb, _gate_pre(xb, wri_ref, h), b_r, b_i, lam, a_ref, b_ref, h)


def _scan_groups(a_ref, b_ref, h_ref, h_in, n_groups):
    row = lax.broadcasted_iota(jnp.int32, (SUBLANES, CB), 0)

    h_prev = h_in
    for g in range(n_groups):
        rows = slice(g * SUBLANES, (g + 1) * SUBLANES)
        a = a_ref[rows, :]
        b = b_ref[rows, :]
        for s in (1, 2, 4):
            keep = row >= s
            a_sh = jnp.where(keep, pltpu.roll(a, s, 0), 1.0)
            b_sh = jnp.where(keep, pltpu.roll(b, s, 0), 0.0)
            b = a * b_sh + b
            a = a * a_sh
        h = a * h_prev + b
        h_ref[rows, :] = h
        h_prev = jnp.broadcast_to(h[SUBLANES - 1:SUBLANES, :], (SUBLANES, CB))
    return h_prev


def _norm(x_ref, g_ref, u_ref):
    @pl.when(pl.program_id(1) == 0)
    def _():
        u_ref[...] = _rms(x_ref[...], g_ref[...]).astype(BF16)

    return u_ref[...]


def _gates(u, wga_ref, wgb_ref, sga_ref, sgb_ref):
    sga_ref[...] = _sigmoid(
        jnp.dot(u, wga_ref[...], preferred_element_type=F32)).astype(BF16)
    sgb_ref[...] = _sigmoid(
        jnp.dot(u, wgb_ref[...], preferred_element_type=F32)).astype(BF16)


def _cast_operands(chunks, src_layer):
    in_specs, out_specs, out_shape, args = [], [], [], []
    for w, row_blocks, col_blocks, idx, *rest in chunks:
        out_idx = rest[0] if rest else idx
        rows, cols = w.shape[1] // row_blocks, w.shape[2] // col_blocks
        assert (rows * row_blocks, cols * col_blocks) == w.shape[1:], w.shape
        assert rows % BF16_ROWS == 0 and cols % LANES == 0, (w.shape, rows, cols)
        in_specs.append(pl.BlockSpec(
            (None, rows, cols), lambda *g, idx=idx: (src_layer,) + idx(*g)))
        out_specs.append(pl.BlockSpec(
            (None, rows, cols), lambda *g, idx=out_idx: (0,) + idx(*g)))
        out_shape.append(jax.ShapeDtypeStruct((1,) + w.shape[1:], BF16))
        args.append(w)
    return in_specs, out_specs, out_shape, args


def _paired_gate_up_chunk(w_gu, step):
    nh = FFN_HIDDEN // TH_MLP
    src = lambda *g: (step(*g) % GU_ROW_BLOCKS, step(*g) // GU_ROW_BLOCKS)

    def dst(*g):
        r, b = src(*g)
        return r, jnp.where(b < nh, 2 * b, 2 * (b - nh) + 1)

    return (w_gu, GU_ROW_BLOCKS, 2 * nh, src, dst)


def _cast_blocks(cast_in, cast_out):
    for src, dst in zip(cast_in, cast_out):
        dst[...] = src[...].astype(BF16)


PROJ_MIX_INPUTS = 16
PROJ_MIX_OUTPUTS = 7
N_W_IN_SLICES = 7


def _proj_mix_prompt_kernel(tiles_per_seq, n_cast, *refs):
    (x_ref, g_ref, wabg_ref, wacg_ref, wax_ref, wbx_ref, wbg_ref, wga_ref, wgb_ref,
     wa_ref, wb_ref, bias_ref, wri_ref, br_ref, bi_ref, lam_ref) = refs[:PROJ_MIX_INPUTS]
    refs = refs[PROJ_MIX_INPUTS:]
    cast_in, refs = refs[:n_cast], refs[n_cast:]
    ya_ref, yb_ref, sga_ref, sgb_ref, na_ref, nb_ref, hl_ref = refs[:PROJ_MIX_OUTPUTS]
    refs = refs[PROJ_MIX_OUTPUTS:]
    cast_out, refs = refs[:n_cast], refs[n_cast:]
    (u_ref, exta_ref, extb_ref, taila_ref, tailb_ref, hc_ref, a_buf, b_buf, h_buf) = refs
    c = pl.program_id(1)
    _norm(x_ref, g_ref, u_ref)
    dot = lambda w_ref: jnp.dot(u_ref[...], w_ref[...], preferred_element_type=F32)

    @pl.when(pl.program_id(0) % tiles_per_seq == 0)
    def _():
        taila_ref[c] = jnp.zeros((SUBLANES, CA), F32)
        tailb_ref[c] = jnp.zeros((SUBLANES, CB), F32)
        hc_ref[c] = jnp.zeros((SUBLANES, CB), F32)

    pre_ga = dot(wga_ref)
    _cast_blocks(cast_in, cast_out)
    bx = dot(wbx_ref)
    sga_ref[...] = _sigmoid(pre_ga).astype(BF16)

    pre_gb = dot(wgb_ref)
    extb_ref[0:SUBLANES, :] = tailb_ref[c]
    extb_ref[SUBLANES:, :] = bx
    wb = wb_ref[...]
    xb = (extb_ref[SUBLANES - 3:SUBLANES - 3 + TM, :] * wb[0:1]
          + extb_ref[SUBLANES - 2:SUBLANES - 2 + TM, :] * wb[1:2]
          + extb_ref[SUBLANES - 1:SUBLANES - 1 + TM, :] * wb[2:3]
          + bx * wb[3:4]) + bias_ref[...]
    tail_b = extb_ref[TM:TM + SUBLANES, :]
    nb_ref[...] = tail_b
    tailb_ref[c] = tail_b

    acg = dot(wacg_ref)
    sgb_ref[...] = _sigmoid(pre_gb).astype(BF16)
    ax = dot(wax_ref)
    heads = range(CB // HEAD)
    pres = [_gate_pre(xb, wri_ref, h) for h in heads]
    coeffs = functools.partial(_lru_coeffs_head, xb, b_r=br_ref[...], b_i=bi_ref[...],
                               lam=lam_ref[...], a_ref=a_buf, b_ref=b_buf)
    coeffs(pre=pres[0], h=0)
    bg = dot(wbg_ref)
    coeffs(pre=pres[1], h=1)
    coeffs(pre=pres[2], h=2)

    z = acg * ax
    exta_ref[0:SUBLANES, :] = taila_ref[c]
    exta_ref[SUBLANES:, :] = z
    wa = wa_ref[...]
    conv = (exta_ref[SUBLANES - 2:SUBLANES - 2 + TM, :] * wa[0:1]
            + exta_ref[SUBLANES - 1:SUBLANES - 1 + TM, :] * wa[1:2]
            + z * wa[2:3])
    tail_a = exta_ref[TM:TM + SUBLANES, :]
    na_ref[...] = tail_a
    taila_ref[c] = tail_a

    coeffs(pre=pres[3], h=3)
    abg = dot(wabg_ref)
    h_out = _scan_groups(a_buf, b_buf, h_buf, hc_ref[c], TM // SUBLANES)
    hc_ref[c] = h_out
    hl_ref[...] = h_out
    yb_ref[...] = (h_buf[...] * jax.nn.gelu(bg)).astype(BF16)
    ya_ref[...] = (abg * conv).astype(BF16)


def _proj_mix_sample_kernel(nb, nsteps, x_ref, g_ref,
                            wabg_ref, wacg_ref, wax_ref, wbx_ref, wbg_ref, wga_ref, wgb_ref,
                            wa_ref, wb_ref, bias_ref, wri_ref, br_ref, bi_ref, lam_ref,
                            sa0_ref, sa1_ref, sb0_ref, sb1_ref, sb2_ref, h0_ref,
                            ya_ref, yb_ref, sga_ref, sgb_ref, na_ref, nb_ref, hl_ref,
                            u_ref, a_buf, b_buf):
    u = _norm(x_ref, g_ref, u_ref)
    _gates(u, wga_ref, wgb_ref, sga_ref, sgb_ref)
    dot = lambda w_ref: jnp.dot(u, w_ref[...], preferred_element_type=F32)
    slab = lambda t: slice(t * nb, (t + 1) * nb)

    z = dot(wacg_ref) * dot(wax_ref)
    abg = dot(wabg_ref)
    ins = [sa0_ref[...], sa1_ref[...]] + [z[slab(t), :] for t in range(nsteps)]
    wa = wa_ref[...]
    for t in range(nsteps):
        conv = ins[t] * wa[0:1] + ins[t + 1] * wa[1:2] + ins[t + 2] * wa[2:3]
        ya_ref[slab(t), :] = (abg[slab(t), :] * conv).astype(BF16)
    for k in range(CONV_A_W - 1):
        na_ref[k] = ins[nsteps + k]

    bx = dot(wbx_ref)
    ins = ([sb0_ref[...], sb1_ref[...], sb2_ref[...]]
           + [bx[slab(t), :] for t in range(nsteps)])
    wb = wb_ref[...]
    xb = jnp.concatenate(
        [(ins[t] * wb[0:1] + ins[t + 1] * wb[1:2] + ins[t + 2] * wb[2:3]
          + ins[t + 3] * wb[3:4]) + bias_ref[...] for t in range(nsteps)], axis=0)
    for k in range(CONV_B_W - 1):
        nb_ref[k] = ins[nsteps + k]

    _lru_coeffs(xb, wri_ref, br_ref[...], bi_ref[...], lam_ref[...], a_buf, b_buf)
    gate = jax.nn.gelu(dot(wbg_ref))
    h = h0_ref[...]
    for t in range(nsteps):
        h = a_buf[slab(t), :] * h + b_buf[slab(t), :]
        yb_ref[slab(t), :] = (h * gate[slab(t), :]).astype(BF16)
    hl_ref[...] = h


def _mix_weight_specs(layer, blk):
    w_col = lambda width, off: pl.BlockSpec(
        (None, D_MODEL, width), lambda *g: (0, 0, off // width + blk(*g)))
    vec_b = pl.BlockSpec((None, 1, CB), lambda *g: (layer, 0, blk(*g)))
    return [
        w_col(CA, OFF_A_BG), w_col(CA, OFF_A_CG), w_col(CA, OFF_A_X),
        w_col(CB, OFF_B_X), w_col(CB, OFF_B_GATE),
        w_col(CB, OFF_GATE_A), w_col(CB, OFF_GATE_B),
        pl.BlockSpec((None, CONV_A_W, CA), lambda *g: (layer, 0, blk(*g))),
        pl.BlockSpec((None, CONV_B_W, CB), lambda *g: (layer, 0, blk(*g))),
        vec_b,
        pl.BlockSpec((None, CB // HEAD, HEAD, 2 * HEAD), lambda *g: (layer, blk(*g), 0, 0)),
        vec_b, vec_b, vec_b,
    ]


def _proj_mix_prompt(x, g, w_in16, wa, wb, bias, wri, br, bi, lam, layer, seq, n_tiles,
                     cast_same=None):
    m = n_tiles * TM
    tiles_per_seq = seq // TM
    in_specs = [
        pl.BlockSpec((TM, D_MODEL), lambda i, c: (i, 0)),
        pl.BlockSpec((None, 1, D_MODEL), lambda i, c: (layer, 0, 0)),
    ] + _mix_weight_specs(layer, lambda i, c: c)
    args = [x, g] + [w_in16] * N_W_IN_SLICES + [wa, wb, bias, wri, br, bi, lam]
    out_row = lambda width: pl.BlockSpec((TM, width), lambda i, c: (i, c))
    state = lambda width: pl.BlockSpec((None, SUBLANES, width), lambda i, c: (i, 0, c))
    out_specs = [out_row(CA), out_row(CB), out_row(CB), out_row(CB),
                 state(CA), state(CB), state(CB)]
    out_shape = [
        jax.ShapeDtypeStruct((m, D_A), BF16),
        jax.ShapeDtypeStruct((m, D_B), BF16),
        jax.ShapeDtypeStruct((m, D_MODEL), BF16),
        jax.ShapeDtypeStruct((m, D_MODEL), BF16),
        jax.ShapeDtypeStruct((n_tiles, SUBLANES, D_A), F32),
        jax.ShapeDtypeStruct((n_tiles, SUBLANES, D_B), F32),
        jax.ShapeDtypeStruct((n_tiles, SUBLANES, D_B), F32),
    ]
    assert (len(in_specs), len(out_specs)) == (PROJ_MIX_INPUTS, PROJ_MIX_OUTPUTS)
    n_cast = 0
    if cast_same is not None:
        w_gu, *others = cast_same
        assert n_tiles * MIX_BLOCKS == GU_ROW_BLOCKS * 2 * (FFN_HIDDEN // TH_MLP)
        chunks = ([_paired_gate_up_chunk(w_gu, lambda i, c: i * MIX_BLOCKS + c)]
                  + [(w, n_tiles, MIX_BLOCKS, lambda i, c: (i, c)) for w in others])
        n_cast = len(chunks)
        c_in, c_out, c_shape, c_args = _cast_operands(chunks, layer)
        in_specs, args = in_specs + c_in, args + c_args
        out_specs, out_shape = out_specs + c_out, out_shape + c_shape
    scratch = [
        pltpu.VMEM((TM, D_MODEL), BF16),
        pltpu.VMEM((SUBLANES + TM, CA), F32),
        pltpu.VMEM((SUBLANES + TM, CB), F32),
        pltpu.VMEM((MIX_BLOCKS, SUBLANES, CA), F32),
        pltpu.VMEM((MIX_BLOCKS, SUBLANES, CB), F32),
        pltpu.VMEM((MIX_BLOCKS, SUBLANES, CB), F32),
        pltpu.VMEM((TM, CB), F32),
        pltpu.VMEM((TM, CB), F32),
        pltpu.VMEM((TM, CB), F32),
    ]
    return pl.pallas_call(
        functools.partial(_proj_mix_prompt_kernel, tiles_per_seq, n_cast),
        grid=(n_tiles, MIX_BLOCKS),
        in_specs=in_specs,
        out_specs=out_specs,
        out_shape=out_shape,
        scratch_shapes=scratch,
        compiler_params=_params(2),
        name="proj_mix_prompt_cast" if n_cast else "proj_mix_prompt",
    )(*args)


def _proj_mix_sample(x, g, w_in16, wa, wb, bias, wri, br, bi, lam, layer, states,
                     row_tile):
    sa, sb, h0, nb, nsteps = states
    m = nb * nsteps
    assert m == TM
    st_a = lambda k: pl.BlockSpec(
        (None, nb, CA), lambda i, c: (layer, 0, k * MIX_BLOCKS + c))
    st_b = lambda k: pl.BlockSpec(
        (None, nb, CB), lambda i, c: (layer, 0, k * MIX_BLOCKS + c))
    in_specs = [
        pl.BlockSpec((TM, D_MODEL), lambda i, c: (row_tile + i, 0)),
        pl.BlockSpec((None, 1, D_MODEL), lambda i, c: (layer, 0, 0)),
    ] + _mix_weight_specs(layer, lambda i, c: c) + [
        st_a(0), st_a(1), st_b(0), st_b(1), st_b(2),
        pl.BlockSpec((None, nb, CB), lambda i, c: (layer, 0, c)),
    ]
    args = ([x, g] + [w_in16] * N_W_IN_SLICES
            + [wa, wb, bias, wri, br, bi, lam, sa, sa, sb, sb, sb, h0])
    out_specs = [
        pl.BlockSpec((TM, CA), lambda i, c: (i, c)),
        pl.BlockSpec((TM, CB), lambda i, c: (i, c)),
        pl.BlockSpec((TM, CB), lambda i, c: (i, c)),
        pl.BlockSpec((TM, CB), lambda i, c: (i, c)),
        pl.BlockSpec((CONV_A_W - 1, nb, CA), lambda i, c: (0, 0, c)),
        pl.BlockSpec((CONV_B_W - 1, nb, CB), lambda i, c: (0, 0, c)),
        pl.BlockSpec((nb, CB), lambda i, c: (0, c)),
    ]
    out_shape = [
        jax.ShapeDtypeStruct((m, D_A), BF16),
        jax.ShapeDtypeStruct((m, D_B), BF16),
        jax.ShapeDtypeStruct((m, D_MODEL), BF16),
        jax.ShapeDtypeStruct((m, D_MODEL), BF16),
        jax.ShapeDtypeStruct((CONV_A_W - 1, nb, D_A), F32),
        jax.ShapeDtypeStruct((CONV_B_W - 1, nb, D_B), F32),
        jax.ShapeDtypeStruct((nb, D_B), F32),
    ]
    return pl.pallas_call(
        functools.partial(_proj_mix_sample_kernel, nb, nsteps),
        grid=(m // TM, MIX_BLOCKS),
        in_specs=in_specs,
        out_specs=out_specs,
        out_shape=out_shape,
        scratch_shapes=[pltpu.VMEM((TM, D_MODEL), BF16),
                        pltpu.VMEM((TM, CB), F32), pltpu.VMEM((TM, CB), F32)],
        compiler_params=_params(2),
        name="proj_mix_sample",
    )(*args)


def _merge_kernel(n_prompt, split_x, *refs):
    (ya_p, ya_s, yb_p, yb_s, sga_p, sga_s, sgb_p, sgb_s, woa_ref, wob_ref, wo_ref) = refs[:11]
    x_refs = refs[11:13] if split_x else refs[11:12]
    g_ref, gffn_ref, o_ref, v_ref = refs[11 + len(x_refs):]
    is_prompt = pl.program_id(0) < n_prompt
    pick = lambda p_ref, s_ref: jnp.where(is_prompt, p_ref[...], s_ref[...])
    ya, yb = pick(ya_p, ya_s), pick(yb_p, yb_s)
    sga, sgb = pick(sga_p, sga_s), pick(sgb_p, sgb_s)
    x = pick(*x_refs) if split_x else x_refs[0][...]
    tc = TC_MERGE
    mix = None
    for jb in range(D_MODEL // tc):
        cols = slice(jb * tc, (jb + 1) * tc)
        y_a = jnp.dot(ya, woa_ref[:, cols], preferred_element_type=F32)
        y_b = jnp.dot(yb, wob_ref[:, cols], preferred_element_type=F32)
        merged = (sga[:, cols].astype(F32) * y_a
                  + sgb[:, cols].astype(F32) * y_b).astype(BF16)
        part = jnp.dot(merged, wo_ref[cols, :], preferred_element_type=F32)
        mix = part if mix is None else mix + part
    x_out = x + _rms(mix, g_ref[...])
    o_ref[...] = x_out
    v_ref[...] = _rms(x_out, gffn_ref[...]).astype(BF16)


def _merge(ya, yb, sga, sgb, woa, wob, wo, x, g, g_ffn, layer):
    tm = TM_RESIDENT
    n_prompt, n_sample = ya[0].shape[0] // tm, ya[1].shape[0] // tm
    split_x = isinstance(x, tuple)
    prompt = lambda width: pl.BlockSpec(
        (tm, width), lambda i: (jnp.minimum(i, n_prompt - 1), 0))
    sample = lambda width: pl.BlockSpec(
        (tm, width), lambda i: (jnp.maximum(i - n_prompt, 0), 0))
    rows = lambda width: pl.BlockSpec((tm, width), lambda i: (i, 0))
    resident = lambda k, n: pl.BlockSpec(
        (None, k, n), lambda i: (0, 0, 0), pipeline_mode=pl.Buffered(1))
    vec = pl.BlockSpec((None, 1, D_MODEL), lambda i: (layer, 0, 0))
    x_specs = [prompt(D_MODEL), sample(D_MODEL)] if split_x else [rows(D_MODEL)]
    m = (n_prompt + n_sample) * tm
    return pl.pallas_call(
        functools.partial(_merge_kernel, n_prompt, split_x),
        grid=(n_prompt + n_sample,),
        in_specs=[
            prompt(D_A), sample(D_A), prompt(D_B), sample(D_B),
            prompt(D_MODEL), sample(D_MODEL), prompt(D_MODEL), sample(D_MODEL),
            resident(D_A, D_MODEL), resident(D_B, D_MODEL), resident(D_MODEL, D_MODEL),
        ] + x_specs + [vec, vec],
        out_specs=[rows(D_MODEL), rows(D_MODEL)],
        out_shape=[jax.ShapeDtypeStruct((m, D_MODEL), F32),
                   jax.ShapeDtypeStruct((m, D_MODEL), BF16)],
        compiler_params=_params(1),
        name="merge",
    )(*ya, *yb, *sga, *sgb, woa, wob, wo, *(x if split_x else (x,)), g, g_ffn)


def _mlp_up_kernel(n_cast, v_ref, wgu_ref, *refs):
    cast_in, h_ref, cast_out = refs[:n_cast], refs[n_cast], refs[n_cast + 1:]
    gu = jnp.dot(v_ref[...], wgu_ref[...], preferred_element_type=F32)
    _cast_blocks(cast_in, cast_out)
    g, u = gu[:, :TH_MLP], gu[:, TH_MLP:]
    h_ref[...] = ((g * _sigmoid(g)) * u).astype(BF16)


def _mlp_up(v, wgu, layer, cast_next=None):
    m = v.shape[0]
    nh = FFN_HIDDEN // TH_MLP
    n_i = m // TM
    in_specs = [
        pl.BlockSpec((TM, D_MODEL), lambda j, i: (i, 0)),
        pl.BlockSpec((None, D_MODEL, 2 * TH_MLP), lambda j, i: (0, 0, j)),
    ]
    out_specs = [pl.BlockSpec((TM, TH_MLP), lambda j, i: (i, j))]
    out_shape = [jax.ShapeDtypeStruct((m, FFN_HIDDEN), BF16)]
    args = [v, wgu]
    n_cast = 0
    if cast_next is not None:
        w_in, w_gu, w_down, w_oa, w_ob, w_o = cast_next
        nc = CAST_TILES
        assert nc <= n_i and nh * nc == GU_ROW_BLOCKS * 2 * nh
        ci = lambda i: jnp.minimum(i, nc - 1)
        chunks = [
            (w_in, nc, nh, lambda j, i: (ci(i), j)),
            _paired_gate_up_chunk(w_gu, lambda j, i: j * nc + ci(i)),
            (w_down, nh, nc, lambda j, i: (j, ci(i))),
            (w_oa, nc, nh, lambda j, i: (ci(i), j)),
            (w_ob, nc, nh, lambda j, i: (ci(i), j)),
            (w_o, nc, nh, lambda j, i: (ci(i), j)),
        ]
        n_cast = len(chunks)
        c_in, c_out, c_shape, c_args = _cast_operands(chunks, layer + 1)
        in_specs, args = in_specs + c_in, args + c_args
        out_specs, out_shape = out_specs + c_out, out_shape + c_shape
    outs = pl.pallas_call(
        functools.partial(_mlp_up_kernel, n_cast),
        grid=(nh, n_i),
        in_specs=in_specs,
        out_specs=out_specs,
        out_shape=out_shape,
        compiler_params=_params(2),
        name="mlp_up_cast" if n_cast else "mlp_up",
    )(*args)
    return outs[0], tuple(outs[1:])


def _mlp_down_kernel(n_prompt, h_ref, wd_ref, x_ref, g_ref, *o_refs):
    f = jnp.dot(h_ref[...], wd_ref[...], preferred_element_type=F32)
    out = x_ref[...] + _rms(f, g_ref[...])
    if n_prompt is None:
        o_refs[0][...] = out
    else:
        @pl.when(pl.program_id(0) < n_prompt)
        def _():
            o_refs[0][...] = out

        @pl.when(pl.program_id(0) >= n_prompt)
        def _():
            o_refs[1][...] = out


def _mlp_down(hidden, wd, x, gpost, layer, split_rows=None):
    m = x.shape[0]
    tm = TM_RESIDENT
    rows = lambda width: pl.BlockSpec((tm, width), lambda i: (i, 0))
    if split_rows is None:
        n_prompt = None
        out_specs = rows(D_MODEL)
        out_shape = jax.ShapeDtypeStruct((m, D_MODEL), F32)
    else:
        n_prompt = split_rows // tm
        out_specs = [
            pl.BlockSpec((tm, D_MODEL), lambda i: (jnp.minimum(i, n_prompt - 1), 0)),
            pl.BlockSpec((tm, D_MODEL), lambda i: (jnp.maximum(i - n_prompt, 0), 0)),
        ]
        out_shape = [jax.ShapeDtypeStruct((split_rows, D_MODEL), F32),
                     jax.ShapeDtypeStruct((m - split_rows, D_MODEL), F32)]
    return pl.pallas_call(
        functools.partial(_mlp_down_kernel, n_prompt),
        grid=(m // tm,),
        in_specs=[
            rows(FFN_HIDDEN),
            pl.BlockSpec((None, FFN_HIDDEN, D_MODEL), lambda i: (0, 0, 0),
                         pipeline_mode=pl.Buffered(1)),
            rows(D_MODEL),
            pl.BlockSpec((None, 1, D_MODEL), lambda i: (layer, 0, 0)),
        ],
        out_specs=out_specs,
        out_shape=out_shape,
        compiler_params=_params(1),
        name="mlp_down",
    )(hidden, wd, x, gpost)


def kernel(x_prompt, x_sample, state_conv_a, state_conv_b, state_lru_h, w_in, conv_a_w,
           w_out_a, conv_b_w, conv_b_bias, w_r, b_r, w_i, b_i, lru_lambda, w_out_b, w_o,
           norm_pre_mix, norm_post_mix, norm_pre_ffn, norm_post_ffn, w_gate_up, w_down):
    batch, seq, _ = x_prompt.shape
    nb, nsteps, _ = x_sample.shape
    depth = w_in.shape[0]
    assert seq % TM == 0 and nb * nsteps == TM
    assert nsteps >= CONV_B_W - 1

    big_w = (w_in, w_gate_up, w_down, w_out_a, w_out_b, w_o)
    w16 = (w_in[:1].astype(BF16),)
    wri16 = jnp.concatenate([w_r, w_i], axis=-1).astype(BF16)
    row = lambda v: v.reshape(depth, 1, v.shape[-1])
    bias, br, bi, lam = row(conv_b_bias), row(b_r), row(b_i), row(lru_lambda)
    g_pre_mix, g_post_mix = row(norm_pre_mix), row(norm_post_mix)
    g_pre_ffn, g_post_ffn = row(norm_pre_ffn), row(norm_post_ffn)
    sa = state_conv_a.reshape(depth, nb, (CONV_A_W - 1) * D_A)
    sb = state_conv_b.reshape(depth, nb, (CONV_B_W - 1) * D_B)

    xp = x_prompt.reshape(batch * seq, D_MODEL)
    xs = x_sample.transpose(1, 0, 2).reshape(nsteps * nb, D_MODEL)

    n_prompt_rows = batch * seq
    x_all = None
    pa, pb, ph, s_a, s_b, s_h = [], [], [], [], [], []
    for l in range(depth):
        mix_w = (w16[0], conv_a_w, conv_b_w, bias, wri16, br, bi, lam, l)
        ya_p, yb_p, sga_p, sgb_p, na_p, nb_p, hl_p, *cast = _proj_mix_prompt(
            xp if l == 0 else x_all, g_pre_mix, *mix_w, seq, n_prompt_rows // TM,
            cast_same=big_w[1:] if l == 0 else None)
        w16 = w16 + tuple(cast)
        w_in16, w_gu16, w_down16, w_out_a16, w_out_b16, w_o16 = w16
        ya_s, yb_s, sga_s, sgb_s, na_s, nb_s, hl_s = _proj_mix_sample(
            xs if l == 0 else x_all, g_pre_mix, *mix_w,
            (sa, sb, state_lru_h, nb, nsteps), 0 if l == 0 else n_prompt_rows // TM)
        x_all, v_all = _merge(
            (ya_p, ya_s), (yb_p, yb_s), (sga_p, sga_s), (sgb_p, sgb_s),
            w_out_a16, w_out_b16, w_o16, (xp, xs) if l == 0 else x_all,
            g_post_mix, g_pre_ffn, l)
        h_all, w16 = _mlp_up(v_all, w_gu16, l, cast_next=big_w if l + 1 < depth else None)
        if l + 1 < depth:
            x_all = _mlp_down(h_all, w_down16, x_all, g_post_ffn, l)
        else:
            xp, xs = _mlp_down(h_all, w_down16, x_all, g_post_ffn, l,
                               split_rows=n_prompt_rows)
        last = slice(seq // TM - 1, None, seq // TM)
        pa.append(na_p[last, SUBLANES - (CONV_A_W - 1):, :])
        pb.append(nb_p[last, SUBLANES - (CONV_B_W - 1):, :])
        ph.append(hl_p[last, 0, :])
        s_a.append(na_s)
        s_b.append(nb_s)
        s_h.append(hl_s)

    y_prompt = xp.reshape(batch, seq, D_MODEL)
    y_sample = xs.reshape(nsteps, nb, D_MODEL).transpose(1, 0, 2)
    return (y_prompt, y_sample, jnp.stack(pa), jnp.stack(pb), jnp.stack(ph),
            jnp.stack(s_a).transpose(0, 2, 1, 3), jnp.stack(s_b).transpose(0, 2, 1, 3),
            jnp.stack(s_h))
```

```python
import functools

import jax
import jax.numpy as jnp
from jax import lax
from jax.experimental import pallas as pl
from jax.experimental.pallas import tpu as pltpu

F32 = jnp.float32
BF16 = jnp.bfloat16

D_MODEL = 2048
D_A = D_MODEL // 2
D_B = D_MODEL
CONV_A_W = 3
CONV_B_W = 4
HEAD = 128
C_LRU = 8.0
LOG2_E = 1.4426950408889634
FFN_HIDDEN = 5632
EPS = 1e-6
IN_COLS = 3 * D_A + 2 * D_B + 2 * D_MODEL

OFF_A_BG, OFF_A_CG, OFF_A_X = 0, D_A, 2 * D_A
OFF_B_X = 3 * D_A
OFF_B_GATE = OFF_B_X + D_B
OFF_GATE_A = OFF_B_GATE + D_B
OFF_GATE_B = OFF_GATE_A + D_MODEL

SUBLANES = 8
BF16_ROWS = 16
LANES = 128
TM = 512
TM_RESIDENT = 256
TH_MLP = 1408
CAST_TILES = 16
GU_ROW_BLOCKS = 8
MIX_BLOCKS = 4
CA = D_A // MIX_BLOCKS
CB = D_B // MIX_BLOCKS
V7X_VMEM_BYTES = 64 * 1024 * 1024
VMEM_LIMIT = V7X_VMEM_BYTES * 7 // 8


def _params(n_axes):
    return pltpu.CompilerParams(
        dimension_semantics=("arbitrary",) * n_axes,
        vmem_limit_bytes=VMEM_LIMIT)


def _rms(x, g):
    var = jnp.mean(x * x, axis=-1, keepdims=True)
    return x * lax.rsqrt(var + EPS) * g


def _sigmoid(x):
    return 0.5 * jnp.tanh(0.5 * x) + 0.5


def _log_sigmoid(x):
    return jnp.minimum(x, 0.0) - jnp.log1p(jnp.exp(-jnp.abs(x)))


def _head(h):
    return slice(h * HEAD, (h + 1) * HEAD)


def _gate_pre(xb, wri_ref, h):
    return jnp.dot(xb[:, _head(h)].astype(BF16), wri_ref[h], preferred_element_type=F32)


def _lru_coeffs_head(xb, pre, b_r, b_i, lam, a_ref, b_ref, h):
    sl = _head(h)
    r = _sigmoid(pre[:, :HEAD] + b_r[:, sl])
    i = _sigmoid(pre[:, HEAD:] + b_i[:, sl])
    a = jnp.exp2(r * ((C_LRU * LOG2_E) * _log_sigmoid(lam[:, sl])))
    y = 1.0 - a * a
    mult = jnp.where(y > 0.0, y * lax.rsqrt(y), 0.0)
    a_ref[:, sl] = a
    b_ref[:, sl] = mult * i * xb[:, sl]


def _lru_coeffs(xb, wri_ref, b_r, b_i, lam, a_ref, b_ref):
    for h in range(CB // HEAD):
        _lru_coeffs_head(xb, _gate_pre(xb, wri_ref, h), b_r, b_i, lam, a_ref, b_ref, h)


def _scan_groups(a_ref, b_ref, h_ref, h_in, n_groups):
    row = lax.broadcasted_iota(jnp.int32, (SUBLANES, CB), 0)

    h_prev = h_in
    for g in range(n_groups):
        rows = slice(g * SUBLANES, (g + 1) * SUBLANES)
        a = a_ref[rows, :]
        b = b_ref[rows, :]
        for s in (1, 2, 4):
            keep = row >= s
            a_sh = jnp.where(keep, pltpu.roll(a, s, 0), 1.0)
            b_sh = jnp.where(keep, pltpu.roll(b, s, 0), 0.0)
            b = a * b_sh + b
            a = a * a_sh
        h = a * h_prev + b
        h_ref[rows, :] = h
        h_prev = jnp.broadcast_to(h[SUBLANES - 1:SUBLANES, :], (SUBLANES, CB))
    return h_prev


def _norm(x_ref, g_ref, u_ref):
    @pl.when(pl.program_id(1) == 0)
    def _():
        u_ref[...] = _rms(x_ref[...], g_ref[...]).astype(BF16)

    return u_ref[...]


def _gates(u, wga_ref, wgb_ref, sga_ref, sgb_ref):
    sga_ref[...] = _sigmoid(
        jnp.dot(u, wga_ref[...], preferred_element_type=F32)).astype(BF16)
    sgb_ref[...] = _sigmoid(
        jnp.dot(u, wgb_ref[...], preferred_element_type=F32)).astype(BF16)


def _cast_operands(chunks, src_layer):
    in_specs, out_specs, out_shape, args = [], [], [], []
    for w, row_blocks, col_blocks, idx, *rest in chunks:
        out_idx = rest[0] if rest else idx
        rows, cols = w.shape[1] // row_blocks, w.shape[2] // col_blocks
        assert (rows * row_blocks, cols * col_blocks) == w.shape[1:], w.shape
        assert rows % BF16_ROWS == 0 and cols % LANES == 0, (w.shape, rows, cols)
        in_specs.append(pl.BlockSpec(
            (None, rows, cols), lambda *g, idx=idx: (src_layer,) + idx(*g)))
        out_specs.append(pl.BlockSpec(
            (None, rows, cols), lambda *g, idx=out_idx: (0,) + idx(*g)))
        out_shape.append(jax.ShapeDtypeStruct((1,) + w.shape[1:], BF16))
        args.append(w)
    return in_specs, out_specs, out_shape, args


def _paired_gate_up_chunk(w_gu, step):
    nh = FFN_HIDDEN // TH_MLP
    src = lambda *g: (step(*g) % GU_ROW_BLOCKS, step(*g) // GU_ROW_BLOCKS)

    def dst(*g):
        r, b = src(*g)
        return r, jnp.where(b < nh, 2 * b, 2 * (b - nh) + 1)

    return (w_gu, GU_ROW_BLOCKS, 2 * nh, src, dst)


def _cast_blocks(cast_in, cast_out):
    for src, dst in zip(cast_in, cast_out):
        dst[...] = src[...].astype(BF16)


PROJ_MIX_INPUTS = 16
PROJ_MIX_OUTPUTS = 7
N_W_IN_SLICES = 7


def _proj_mix_prompt_kernel(tiles_per_seq, n_cast, *refs):
    (x_ref, g_ref, wabg_ref, wacg_ref, wax_ref, wbx_ref, wbg_ref, wga_ref, wgb_ref,
     wa_ref, wb_ref, bias_ref, wri_ref, br_ref, bi_ref, lam_ref) = refs[:PROJ_MIX_INPUTS]
    refs = refs[PROJ_MIX_INPUTS:]
    cast_in, refs = refs[:n_cast], refs[n_cast:]
    ya_ref, yb_ref, sga_ref, sgb_ref, na_ref, nb_ref, hl_ref = refs[:PROJ_MIX_OUTPUTS]
    refs = refs[PROJ_MIX_OUTPUTS:]
    cast_out, refs = refs[:n_cast], refs[n_cast:]
    (u_ref, exta_ref, extb_ref, taila_ref, tailb_ref, hc_ref, a_buf, b_buf, h_buf) = refs
    c = pl.program_id(1)
    _norm(x_ref, g_ref, u_ref)
    dot = lambda w_ref: jnp.dot(u_ref[...], w_ref[...], preferred_element_type=F32)

    @pl.when(pl.program_id(0) % tiles_per_seq == 0)
    def _():
        taila_ref[c] = jnp.zeros((SUBLANES, CA), F32)
        tailb_ref[c] = jnp.zeros((SUBLANES, CB), F32)
        hc_ref[c] = jnp.zeros((SUBLANES, CB), F32)

    pre_ga = dot(wga_ref)
    _cast_blocks(cast_in, cast_out)
    bx = dot(wbx_ref)
    sga_ref[...] = _sigmoid(pre_ga).astype(BF16)

    pre_gb = dot(wgb_ref)
    extb_ref[0:SUBLANES, :] = tailb_ref[c]
    extb_ref[SUBLANES:, :] = bx
    wb = wb_ref[...]
    xb = (extb_ref[SUBLANES - 3:SUBLANES - 3 + TM, :] * wb[0:1]
          + extb_ref[SUBLANES - 2:SUBLANES - 2 + TM, :] * wb[1:2]
          + extb_ref[SUBLANES - 1:SUBLANES - 1 + TM, :] * wb[2:3]
          + bx * wb[3:4]) + bias_ref[...]
    tail_b = extb_ref[TM:TM + SUBLANES, :]
    nb_ref[...] = tail_b
    tailb_ref[c] = tail_b

    acg = dot(wacg_ref)
    sgb_ref[...] = _sigmoid(pre_gb).astype(BF16)
    ax = dot(wax_ref)
    heads = range(CB // HEAD)
    pres = [_gate_pre(xb, wri_ref, h) for h in heads]
    coeffs = functools.partial(_lru_coeffs_head, xb, b_r=br_ref[...], b_i=bi_ref[...],
                               lam=lam_ref[...], a_ref=a_buf, b_ref=b_buf)
    coeffs(pre=pres[0], h=0)
    bg = dot(wbg_ref)
    coeffs(pre=pres[1], h=1)
    coeffs(pre=pres[2], h=2)

    z = acg * ax
    exta_ref[0:SUBLANES, :] = taila_ref[c]
    exta_ref[SUBLANES:, :] = z
    wa = wa_ref[...]
    conv = (exta_ref[SUBLANES - 2:SUBLANES - 2 + TM, :] * wa[0:1]
            + exta_ref[SUBLANES - 1:SUBLANES - 1 + TM, :] * wa[1:2]
            + z * wa[2:3])
    tail_a = exta_ref[TM:TM + SUBLANES, :]
    na_ref[...] = tail_a
    taila_ref[c] = tail_a

    coeffs(pre=pres[3], h=3)
    abg = dot(wabg_ref)
    h_out = _scan_groups(a_buf, b_buf, h_buf, hc_ref[c], TM // SUBLANES)
    hc_ref[c] = h_out
    hl_ref[...] = h_out
    yb_ref[...] = (h_buf[...] * jax.nn.gelu(bg)).astype(BF16)
    ya_ref[...] = (abg * conv).astype(BF16)


def _proj_mix_sample_kernel(nb, nsteps, x_ref, g_ref,
                            wabg_ref, wacg_ref, wax_ref, wbx_ref, wbg_ref, wga_ref, wgb_ref,
                            wa_ref, wb_ref, bias_ref, wri_ref, br_ref, bi_ref, lam_ref,
                            sa0_ref, sa1_ref, sb0_ref, sb1_ref, sb2_ref, h0_ref,
                            ya_ref, yb_ref, sga_ref, sgb_ref, na_ref, nb_ref, hl_ref,
                            u_ref, a_buf, b_buf):
    u = _norm(x_ref, g_ref, u_ref)
    _gates(u, wga_ref, wgb_ref, sga_ref, sgb_ref)
    dot = lambda w_ref: jnp.dot(u, w_ref[...], preferred_element_type=F32)
    slab = lambda t: slice(t * nb, (t + 1) * nb)

    z = dot(wacg_ref) * dot(wax_ref)
    abg = dot(wabg_ref)
    ins = [sa0_ref[...], sa1_ref[...]] + [z[slab(t), :] for t in range(nsteps)]
    wa = wa_ref[...]
    for t in range(nsteps):
        conv = ins[t] * wa[0:1] + ins[t + 1] * wa[1:2] + ins[t + 2] * wa[2:3]
        ya_ref[slab(t), :] = (abg[slab(t), :] * conv).astype(BF16)
    for k in range(CONV_A_W - 1):
        na_ref[k] = ins[nsteps + k]

    bx = dot(wbx_ref)
    ins = ([sb0_ref[...], sb1_ref[...], sb2_ref[...]]
           + [bx[slab(t), :] for t in range(nsteps)])
    wb = wb_ref[...]
    xb = jnp.concatenate(
        [(ins[t] * wb[0:1] + ins[t + 1] * wb[1:2] + ins[t + 2] * wb[2:3]
          + ins[t + 3] * wb[3:4]) + bias_ref[...] for t in range(nsteps)], axis=0)
    for k in range(CONV_B_W - 1):
        nb_ref[k] = ins[nsteps + k]

    _lru_coeffs(xb, wri_ref, br_ref[...], bi_ref[...], lam_ref[...], a_buf, b_buf)
    gate = jax.nn.gelu(dot(wbg_ref))
    h = h0_ref[...]
    for t in range(nsteps):
        h = a_buf[slab(t), :] * h + b_buf[slab(t), :]
        yb_ref[slab(t), :] = (h * gate[slab(t), :]).astype(BF16)
    hl_ref[...] = h


def _mix_weight_specs(layer, blk):
    w_col = lambda width, off: pl.BlockSpec(
        (None, D_MODEL, width), lambda *g: (0, 0, off // width + blk(*g)))
    vec_b = pl.BlockSpec((None, 1, CB), lambda *g: (layer, 0, blk(*g)))
    return [
        w_col(CA, OFF_A_BG), w_col(CA, OFF_A_CG), w_col(CA, OFF_A_X),
        w_col(CB, OFF_B_X), w_col(CB, OFF_B_GATE),
        w_col(CB, OFF_GATE_A), w_col(CB, OFF_GATE_B),
        pl.BlockSpec((None, CONV_A_W, CA), lambda *g: (layer, 0, blk(*g))),
        pl.BlockSpec((None, CONV_B_W, CB), lambda *g: (layer, 0, blk(*g))),
        vec_b,
        pl.BlockSpec((None, CB // HEAD, HEAD, 2 * HEAD), lambda *g: (layer, blk(*g), 0, 0)),
        vec_b, vec_b, vec_b,
    ]


def _proj_mix_prompt(x, g, w_in16, wa, wb, bias, wri, br, bi, lam, layer, seq, n_tiles,
                     cast_same=None):
    m = n_tiles * TM
    tiles_per_seq = seq // TM
    in_specs = [
        pl.BlockSpec((TM, D_MODEL), lambda i, c: (i, 0)),
        pl.BlockSpec((None, 1, D_MODEL), lambda i, c: (layer, 0, 0)),
    ] + _mix_weight_specs(layer, lambda i, c: c)
    args = [x, g] + [w_in16] * N_W_IN_SLICES + [wa, wb, bias, wri, br, bi, lam]
    out_row = lambda width: pl.BlockSpec((TM, width), lambda i, c: (i, c))
    state = lambda width: pl.BlockSpec((None, SUBLANES, width), lambda i, c: (i, 0, c))
    out_specs = [out_row(CA), out_row(CB), out_row(CB), out_row(CB),
                 state(CA), state(CB), state(CB)]
    out_shape = [
        jax.ShapeDtypeStruct((m, D_A), BF16),
        jax.ShapeDtypeStruct((m, D_B), BF16),
        jax.ShapeDtypeStruct((m, D_MODEL), BF16),
        jax.ShapeDtypeStruct((m, D_MODEL), BF16),
        jax.ShapeDtypeStruct((n_tiles, SUBLANES, D_A), F32),
        jax.ShapeDtypeStruct((n_tiles, SUBLANES, D_B), F32),
        jax.ShapeDtypeStruct((n_tiles, SUBLANES, D_B), F32),
    ]
    assert (len(in_specs), len(out_specs)) == (PROJ_MIX_INPUTS, PROJ_MIX_OUTPUTS)
    n_cast = 0
    if cast_same is not None:
        w_gu, *others = cast_same
        assert n_tiles * MIX_BLOCKS == GU_ROW_BLOCKS * 2 * (FFN_HIDDEN // TH_MLP)
        chunks = ([_paired_gate_up_chunk(w_gu, lambda i, c: i * MIX_BLOCKS + c)]
                  + [(w, n_tiles, MIX_BLOCKS, lambda i, c: (i, c)) for w in others])
        n_cast = len(chunks)
        c_in, c_out, c_shape, c_args = _cast_operands(chunks, layer)
        in_specs, args = in_specs + c_in, args + c_args
        out_specs, out_shape = out_specs + c_out, out_shape + c_shape
    scratch = [
        pltpu.VMEM((TM, D_MODEL), BF16),
        pltpu.VMEM((SUBLANES + TM, CA), F32),
        pltpu.VMEM((SUBLANES + TM, CB), F32),
        pltpu.VMEM((MIX_BLOCKS, SUBLANES, CA), F32),
        pltpu.VMEM((MIX_BLOCKS, SUBLANES, CB), F32),
        pltpu.VMEM((MIX_BLOCKS, SUBLANES, CB), F32),
        pltpu.VMEM((TM, CB), F32),
        pltpu.VMEM((TM, CB), F32),
        pltpu.VMEM((TM, CB), F32),
    ]
    return pl.pallas_call(
        functools.partial(_proj_mix_prompt_kernel, tiles_per_seq, n_cast),
        grid=(n_tiles, MIX_BLOCKS),
        in_specs=in_specs,
        out_specs=out_specs,
        out_shape=out_shape,
        scratch_shapes=scratch,
        compiler_params=_params(2),
        name="proj_mix_prompt_cast" if n_cast else "proj_mix_prompt",
    )(*args)


def _proj_mix_sample(x, g, w_in16, wa, wb, bias, wri, br, bi, lam, layer, states,
                     row_tile):
    sa, sb, h0, nb, nsteps = states
    m = nb * nsteps
    assert m == TM
    st_a = lambda k: pl.BlockSpec(
        (None, nb, CA), lambda i, c: (layer, 0, k * MIX_BLOCKS + c))
    st_b = lambda k: pl.BlockSpec(
        (None, nb, CB), lambda i, c: (layer, 0, k * MIX_BLOCKS + c))
    in_specs = [
        pl.BlockSpec((TM, D_MODEL), lambda i, c: (row_tile + i, 0)),
        pl.BlockSpec((None, 1, D_MODEL), lambda i, c: (layer, 0, 0)),
    ] + _mix_weight_specs(layer, lambda i, c: c) + [
        st_a(0), st_a(1), st_b(0), st_b(1), st_b(2),
        pl.BlockSpec((None, nb, CB), lambda i, c: (layer, 0, c)),
    ]
    args = ([x, g] + [w_in16] * N_W_IN_SLICES
            + [wa, wb, bias, wri, br, bi, lam, sa, sa, sb, sb, sb, h0])
    out_specs = [
        pl.BlockSpec((TM, CA), lambda i, c: (i, c)),
        pl.BlockSpec((TM, CB), lambda i, c: (i, c)),
        pl.BlockSpec((TM, CB), lambda i, c: (i, c)),
        pl.BlockSpec((TM, CB), lambda i, c: (i, c)),
        pl.BlockSpec((CONV_A_W - 1, nb, CA), lambda i, c: (0, 0, c)),
        pl.BlockSpec((CONV_B_W - 1, nb, CB), lambda i, c: (0, 0, c)),
        pl.BlockSpec((nb, CB), lambda i, c: (0, c)),
    ]
    out_shape = [
        jax.ShapeDtypeStruct((m, D_A), BF16),
        jax.ShapeDtypeStruct((m, D_B), BF16),
        jax.ShapeDtypeStruct((m, D_MODEL), BF16),
        jax.ShapeDtypeStruct((m, D_MODEL), BF16),
        jax.ShapeDtypeStruct((CONV_A_W - 1, nb, D_A), F32),
        jax.ShapeDtypeStruct((CONV_B_W - 1, nb, D_B), F32),
        jax.ShapeDtypeStruct((nb, D_B), F32),
    ]
    return pl.pallas_call(
        functools.partial(_proj_mix_sample_kernel, nb, nsteps),
        grid=(m // TM, MIX_BLOCKS),
        in_specs=in_specs,
        out_specs=out_specs,
        out_shape=out_shape,
        scratch_shapes=[pltpu.VMEM((TM, D_MODEL), BF16),
                        pltpu.VMEM((TM, CB), F32), pltpu.VMEM((TM, CB), F32)],
        compiler_params=_params(2),
        name="proj_mix_sample",
    )(*args)


def _merge_kernel(n_prompt, split_x, *refs):
    (ya_p, ya_s, yb_p, yb_s, sga_p, sga_s, sgb_p, sgb_s, woa_ref, wob_ref, wo_ref) = refs[:11]
    x_refs = refs[11:13] if split_x else refs[11:12]
    g_ref, gffn_ref, o_ref, v_ref = refs[11 + len(x_refs):]
    is_prompt = pl.program_id(0) < n_prompt
    pick = lambda p_ref, s_ref: jnp.where(is_prompt, p_ref[...], s_ref[...])
    ya, yb = pick(ya_p, ya_s), pick(yb_p, yb_s)
    sga, sgb = pick(sga_p, sga_s), pick(sgb_p, sgb_s)
    x = pick(*x_refs) if split_x else x_refs[0][...]
    y_a = jnp.dot(ya, woa_ref[...], preferred_element_type=F32)
    y_b = jnp.dot(yb, wob_ref[...], preferred_element_type=F32)
    merged = (sga.astype(F32) * y_a + sgb.astype(F32) * y_b).astype(BF16)
    mix = jnp.dot(merged, wo_ref[...], preferred_element_type=F32)
    x_out = x + _rms(mix, g_ref[...])
    o_ref[...] = x_out
    v_ref[...] = _rms(x_out, gffn_ref[...]).astype(BF16)


def _merge(ya, yb, sga, sgb, woa, wob, wo, x, g, g_ffn, layer):
    tm = TM_RESIDENT
    n_prompt, n_sample = ya[0].shape[0] // tm, ya[1].shape[0] // tm
    split_x = isinstance(x, tuple)
    prompt = lambda width: pl.BlockSpec(
        (tm, width), lambda i: (jnp.minimum(i, n_prompt - 1), 0))
    sample = lambda width: pl.BlockSpec(
        (tm, width), lambda i: (jnp.maximum(i - n_prompt, 0), 0))
    rows = lambda width: pl.BlockSpec((tm, width), lambda i: (i, 0))
    resident = lambda k, n: pl.BlockSpec(
        (None, k, n), lambda i: (0, 0, 0), pipeline_mode=pl.Buffered(1))
    vec = pl.BlockSpec((None, 1, D_MODEL), lambda i: (layer, 0, 0))
    x_specs = [prompt(D_MODEL), sample(D_MODEL)] if split_x else [rows(D_MODEL)]
    m = (n_prompt + n_sample) * tm
    return pl.pallas_call(
        functools.partial(_merge_kernel, n_prompt, split_x),
        grid=(n_prompt + n_sample,),
        in_specs=[
            prompt(D_A), sample(D_A), prompt(D_B), sample(D_B),
            prompt(D_MODEL), sample(D_MODEL), prompt(D_MODEL), sample(D_MODEL),
            resident(D_A, D_MODEL), resident(D_B, D_MODEL), resident(D_MODEL, D_MODEL),
        ] + x_specs + [vec, vec],
        out_specs=[rows(D_MODEL), rows(D_MODEL)],
        out_shape=[jax.ShapeDtypeStruct((m, D_MODEL), F32),
                   jax.ShapeDtypeStruct((m, D_MODEL), BF16)],
        compiler_params=_params(1),
        name="merge",
    )(*ya, *yb, *sga, *sgb, woa, wob, wo, *(x if split_x else (x,)), g, g_ffn)


def _mlp_up_kernel(n_cast, v_ref, wgu_ref, *refs):
    cast_in, h_ref, cast_out = refs[:n_cast], refs[n_cast], refs[n_cast + 1:]
    gu = jnp.dot(v_ref[...], wgu_ref[...], preferred_element_type=F32)
    _cast_blocks(cast_in, cast_out)
    g, u = gu[:, :TH_MLP], gu[:, TH_MLP:]
    h_ref[...] = ((g * _sigmoid(g)) * u).astype(BF16)


def _mlp_up(v, wgu, layer, cast_next=None):
    m = v.shape[0]
    nh = FFN_HIDDEN // TH_MLP
    n_i = m // TM
    in_specs = [
        pl.BlockSpec((TM, D_MODEL), lambda j, i: (i, 0)),
        pl.BlockSpec((None, D_MODEL, 2 * TH_MLP), lambda j, i: (0, 0, j)),
    ]
    out_specs = [pl.BlockSpec((TM, TH_MLP), lambda j, i: (i, j))]
    out_shape = [jax.ShapeDtypeStruct((m, FFN_HIDDEN), BF16)]
    args = [v, wgu]
    n_cast = 0
    if cast_next is not None:
        w_in, w_gu, w_down, w_oa, w_ob, w_o = cast_next
        nc = CAST_TILES
        assert nc <= n_i and nh * nc == GU_ROW_BLOCKS * 2 * nh
        ci = lambda i: jnp.minimum(i, nc - 1)
        chunks = [
            (w_in, nc, nh, lambda j, i: (ci(i), j)),
            _paired_gate_up_chunk(w_gu, lambda j, i: j * nc + ci(i)),
            (w_down, nh, nc, lambda j, i: (j, ci(i))),
            (w_oa, nc, nh, lambda j, i: (ci(i), j)),
            (w_ob, nc, nh, lambda j, i: (ci(i), j)),
            (w_o, nc, nh, lambda j, i: (ci(i), j)),
        ]
        n_cast = len(chunks)
        c_in, c_out, c_shape, c_args = _cast_operands(chunks, layer + 1)
        in_specs, args = in_specs + c_in, args + c_args
        out_specs, out_shape = out_specs + c_out, out_shape + c_shape
    outs = pl.pallas_call(
        functools.partial(_mlp_up_kernel, n_cast),
        grid=(nh, n_i),
        in_specs=in_specs,
        out_specs=out_specs,
        out_shape=out_shape,
        compiler_params=_params(2),
        name="mlp_up_cast" if n_cast else "mlp_up",
    )(*args)
    return outs[0], tuple(outs[1:])


def _mlp_down_kernel(n_prompt, h_ref, wd_ref, x_ref, g_ref, *o_refs):
    f = jnp.dot(h_ref[...], wd_ref[...], preferred_element_type=F32)
    out = x_ref[...] + _rms(f, g_ref[...])
    if n_prompt is None:
        o_refs[0][...] = out
    else:
        @pl.when(pl.program_id(0) < n_prompt)
        def _():
            o_refs[0][...] = out

        @pl.when(pl.program_id(0) >= n_prompt)
        def _():
            o_refs[1][...] = out


def _mlp_down(hidden, wd, x, gpost, layer, split_rows=None):
    m = x.shape[0]
    tm = TM_RESIDENT
    rows = lambda width: pl.BlockSpec((tm, width), lambda i: (i, 0))
    if split_rows is None:
        n_prompt = None
        out_specs = rows(D_MODEL)
        out_shape = jax.ShapeDtypeStruct((m, D_MODEL), F32)
    else:
        n_prompt = split_rows // tm
        out_specs = [
            pl.BlockSpec((tm, D_MODEL), lambda i: (jnp.minimum(i, n_prompt - 1), 0)),
            pl.BlockSpec((tm, D_MODEL), lambda i: (jnp.maximum(i - n_prompt, 0), 0)),
        ]
        out_shape = [jax.ShapeDtypeStruct((split_rows, D_MODEL), F32),
                     jax.ShapeDtypeStruct((m - split_rows, D_MODEL), F32)]
    return pl.pallas_call(
        functools.partial(_mlp_down_kernel, n_prompt),
        grid=(m // tm,),
        in_specs=[
            rows(FFN_HIDDEN),
            pl.BlockSpec((None, FFN_HIDDEN, D_MODEL), lambda i: (0, 0, 0),
                         pipeline_mode=pl.Buffered(1)),
            rows(D_MODEL),
            pl.BlockSpec((None, 1, D_MODEL), lambda i: (layer, 0, 0)),
        ],
        out_specs=out_specs,
        out_shape=out_shape,
        compiler_params=_params(1),
        name="mlp_down",
    )(hidden, wd, x, gpost)


def kernel(x_prompt, x_sample, state_conv_a, state_conv_b, state_lru_h, w_in, conv_a_w,
           w_out_a, conv_b_w, conv_b_bias, w_r, b_r, w_i, b_i, lru_lambda, w_out_b, w_o,
           norm_pre_mix, norm_post_mix, norm_pre_ffn, norm_post_ffn, w_gate_up, w_down):
    batch, seq, _ = x_prompt.shape
    nb, nsteps, _ = x_sample.shape
    depth = w_in.shape[0]
    assert seq % TM == 0 and nb * nsteps == TM
    assert nsteps >= CONV_B_W - 1

    big_w = (w_in, w_gate_up, w_down, w_out_a, w_out_b, w_o)
    w16 = (w_in[:1].astype(BF16),)
    wri16 = jnp.concatenate([w_r, w_i], axis=-1).astype(BF16)
    row = lambda v: v.reshape(depth, 1, v.shape[-1])
    bias, br, bi, lam = row(conv_b_bias), row(b_r), row(b_i), row(lru_lambda)
    g_pre_mix, g_post_mix = row(norm_pre_mix), row(norm_post_mix)
    g_pre_ffn, g_post_ffn = row(norm_pre_ffn), row(norm_post_ffn)
    sa = state_conv_a.reshape(depth, nb, (CONV_A_W - 1) * D_A)
    sb = state_conv_b.reshape(depth, nb, (CONV_B_W - 1) * D_B)

    xp = x_prompt.reshape(batch * seq, D_MODEL)
    xs = x_sample.transpose(1, 0, 2).reshape(nsteps * nb, D_MODEL)

    n_prompt_rows = batch * seq
    x_all = None
    pa, pb, ph, s_a, s_b, s_h = [], [], [], [], [], []
    for l in range(depth):
        mix_w = (w16[0], conv_a_w, conv_b_w, bias, wri16, br, bi, lam, l)
        ya_p, yb_p, sga_p, sgb_p, na_p, nb_p, hl_p, *cast = _proj_mix_prompt(
            xp if l == 0 else x_all, g_pre_mix, *mix_w, seq, n_prompt_rows // TM,
            cast_same=big_w[1:] if l == 0 else None)
        w16 = w16 + tuple(cast)
        w_in16, w_gu16, w_down16, w_out_a16, w_out_b16, w_o16 = w16
        ya_s, yb_s, sga_s, sgb_s, na_s, nb_s, hl_s = _proj_mix_sample(
            xs if l == 0 else x_all, g_pre_mix, *mix_w,
            (sa, sb, state_lru_h, nb, nsteps), 0 if l == 0 else n_prompt_rows // TM)
        x_all, v_all = _merge(
            (ya_p, ya_s), (yb_p, yb_s), (sga_p, sga_s), (sgb_p, sgb_s),
            w_out_a16, w_out_b16, w_o16, (xp, xs) if l == 0 else x_all,
            g_post_mix, g_pre_ffn, l)
        h_all, w16 = _mlp_up(v_all, w_gu16, l, cast_next=big_w if l + 1 < depth else None)
        if l + 1 < depth:
            x_all = _mlp_down(h_all, w_down16, x_all, g_post_ffn, l)
        else:
            xp, xs = _mlp_down(h_all, w_down16, x_all, g_post_ffn, l,
                               split_rows=n_prompt_rows)
        last = slice(seq // TM - 1, None, seq // TM)
        pa.append(na_p[last, SUBLANES - (CONV_A_W - 1):, :])
        pb.append(nb_p[last, SUBLANES - (CONV_B_W - 1):, :])
        ph.append(hl_p[last, 0, :])
        s_a.append(na_s)
        s_b.append(nb_s)
        s_h.append(hl_s)

    y_prompt = xp.reshape(batch, seq, D_MODEL)
    y_sample = xs.reshape(nsteps, nb, D_MODEL).transpose(1, 0, 2)
    return (y_prompt, y_sample, jnp.stack(pa), jnp.stack(pb), jnp.stack(ph),
            jnp.stack(s_a).transpose(0, 2, 1, 3), jnp.stack(s_b).transpose(0, 2, 1, 3),
            jnp.stack(s_h))
```
